```python
import jax, jax.numpy as jnp
from jax import lax
import numpy as np

D_MODEL = 4096
BATCH = 2
SEQ = 4096
DEPTH = 2

HEAD_DIM = 128
ROT_DIM = HEAD_DIM // 4
ROPE_THETA = 500000.0
N_HEADS_OUT = D_MODEL // HEAD_DIM
N_HEADS_B_GROUP = N_HEADS_OUT // 4
N_HEADS_A = N_HEADS_OUT - N_HEADS_B_GROUP
B_PATTERNS = ((128, 1), (512, 4), (2048, 16))
N_HEADS_B = N_HEADS_B_GROUP * len(B_PATTERNS)
QKV_COLS = 3 * (N_HEADS_A + N_HEADS_B) * HEAD_DIM
MOBA_BLOCK = 256
MOBA_TOPK = 3
MOBA_QCHUNK = 16
DWIN_BLOCK = 128
GMLP_CHUNK = 128
GMLP_WIDTH = D_MODEL
GMLP_GROUP_DIM = 128
GMLP_GROUPS = GMLP_WIDTH // GMLP_GROUP_DIM
D_FF = 7 * D_MODEL // 2
CONV_WIDTH = 3
NORM_EPS = 1e-5
N_EVEN = (DEPTH + 1) // 2
N_ODD = DEPTH // 2

kernel_name = "hybrid_moba_dilated_gmlp_convffn"

F32 = jnp.float32


def rms_norm(x, g):
    xf = x.astype(F32)
    y = xf * lax.rsqrt(jnp.mean(xf * xf, axis=-1, keepdims=True) + NORM_EPS)
    return (y * g.astype(F32)).astype(x.dtype)


def layer_norm(x, g, b):
    xf = x.astype(F32)
    mu = jnp.mean(xf, axis=-1, keepdims=True)
    xc = xf - mu
    y = xc * lax.rsqrt(jnp.mean(xc * xc, axis=-1, keepdims=True) + NORM_EPS)
    return (y * g.astype(F32) + b.astype(F32)).astype(x.dtype)


def partial_rotary(x, positions):
    half = ROT_DIM // 2
    inv_freq = jnp.power(ROPE_THETA, -jnp.arange(half, dtype=F32) * (2.0 / ROT_DIM))
    ang = positions.astype(F32)[:, None, :, None] * inv_freq
    cos, sin = jnp.cos(ang), jnp.sin(ang)
    xr = x[..., :ROT_DIM].astype(F32)
    x1, x2 = xr[..., :half], xr[..., half:]
    rot = jnp.concatenate([x1 * cos - x2 * sin, x2 * cos + x1 * sin], axis=-1).astype(x.dtype)
    return jnp.concatenate([rot, x[..., ROT_DIM:]], axis=-1)


def moba_attention(q, k, v):
    bsz, nh, s, dh = q.shape
    blk, cq = MOBA_BLOCK, MOBA_QCHUNK
    nb = -(-s // blk)
    sp = nb * blk
    pad = ((0, 0), (0, 0), (0, sp - s), (0, 0))
    bh = bsz * nh
    qf = jnp.pad(q, pad).reshape(bh, sp, dh)
    kf = jnp.pad(k, pad).reshape(bh, sp, dh)
    vf = jnp.pad(v, pad).reshape(bh, sp, dh)
    kb = kf.reshape(bh, nb, blk, dh)
    vb = vf.reshape(bh, nb, blk, dh)
    kmean = jnp.mean(kb.astype(F32), axis=2)
    gate = jnp.einsum('zsd,znd->zsn', qf.astype(F32), kmean)
    qblk = jnp.arange(sp) // blk
    past = jnp.arange(nb)[None, :] < qblk[:, None]
    gate = jnp.where(past[None], gate, -jnp.inf)
    kk = min(MOBA_TOPK, nb)
    _, sel = lax.top_k(gate, kk)
    sel_valid = jnp.arange(kk)[None, :] < qblk[:, None]
    nc = sp // cq
    qc = qf.reshape(bh, nc, cq, dh).transpose(1, 0, 2, 3)
    selc = sel.reshape(bh, nc, cq, kk).transpose(1, 0, 2, 3)
    validc = sel_valid.reshape(nc, cq, kk)
    starts = jnp.arange(nc, dtype=jnp.int32) * cq
    scale = dh ** -0.5

    def one_chunk(args):
        qi, si, vi, t0 = args
        gk = jax.vmap(lambda kz, iz: kz[iz])(kb, si)
        gv = jax.vmap(lambda vz, iz: vz[iz])(vb, si)
        s_sel = jnp.einsum('zcd,zcnjd->zcnj', qi, gk, preferred_element_type=F32) * scale
        s_sel = jnp.where(vi[None, :, :, None], s_sel, -jnp.inf).reshape(bh, cq, kk * blk)
        b0 = (t0 // blk) * blk
        ko = lax.dynamic_slice_in_dim(kf, b0, blk, axis=1)
        vo = lax.dynamic_slice_in_dim(vf, b0, blk, axis=1)
        s_own = jnp.einsum('zcd,zjd->zcj', qi, ko, preferred_element_type=F32) * scale
        causal = (b0 + jnp.arange(blk))[None, :] <= (t0 + jnp.arange(cq))[:, None]
        s_own = jnp.where(causal[None], s_own, -jnp.inf)
        p = jax.nn.softmax(jnp.concatenate([s_sel, s_own], axis=-1), axis=-1).astype(vf.dtype)
        out = jnp.einsum('zcm,zcmd->zcd', p[..., :kk * blk], gv.reshape(bh, cq, kk * blk, dh))
        return out + jnp.einsum('zcj,zjd->zcd', p[..., kk * blk:], vo)

    out = lax.map(one_chunk, (qc, selc, validc, starts))
    out = out.transpose(1, 0, 2, 3).reshape(bh, sp, dh)[:, :s]
    return out.reshape(bsz, nh, s, dh)


def dilated_window_attention(q, k, v, window, dilation):
    bsz, nh, s, dh = q.shape
    n_back = window // dilation
    blk = DWIN_BLOCK
    sub = s // dilation
    nblk = -(-sub // blk)
    lp = nblk * blk

    def to_blocks(t):
        t = t.reshape(bsz, nh, sub, dilation, dh).swapaxes(2, 3)
        t = jnp.pad(t, ((0, 0), (0, 0), (0, 0), (0, lp - sub), (0, 0)))
        return t.reshape(bsz, nh, dilation, nblk, blk, dh)

    def band(t):
        prev = jnp.pad(t, ((0, 0), (0, 0), (0, 0), (1, 0), (0, 0), (0, 0)))[:, :, :, :nblk]
        return jnp.concatenate([prev, t], axis=4)

    qb = to_blocks(q)
    kband = band(to_blocks(k))
    vband = band(to_blocks(v))
    scores = jnp.einsum('bhrnid,bhrnjd->bhrnij', qb, kband, preferred_element_type=F32) * (dh ** -0.5)
    qi = jnp.arange(blk)[:, None]
    kj = jnp.arange(2 * blk)[None, :]
    dist = blk + qi - kj
    in_band = (dist >= 0) & (dist <= n_back)
    not_pad = (jnp.arange(nblk)[:, None, None] > 0) | (kj >= blk)[None]
    mask = in_band[None] & not_pad
    scores = jnp.where(mask, scores, -jnp.inf)
    lse = jax.nn.logsumexp(scores, axis=-1)
    p = jnp.exp(scores - lse[..., None]).astype(v.dtype)
    out = jnp.einsum('bhrnij,bhrnjd->bhrnid', p, vband)
    out = out.reshape(bsz, nh, dilation, lp, dh)[:, :, :, :sub].swapaxes(2, 3).reshape(bsz, nh, s, dh)
    lse = lse.reshape(bsz, nh, dilation, lp)[..., :sub].swapaxes(2, 3).reshape(bsz, nh, s)
    return out, lse


def dilated_mixture(q, k, v):
    outs, lses = [], []
    for g, (window, dilation) in enumerate(B_PATTERNS):
        sl = slice(g * N_HEADS_B_GROUP, (g + 1) * N_HEADS_B_GROUP)
        o, l = dilated_window_attention(q[:, sl], k[:, sl], v[:, sl], window, dilation)
        outs.append(o)
        lses.append(l)
    w = jax.nn.softmax(jnp.stack(lses, axis=0), axis=0)
    return jnp.einsum('gbhs,gbhsd->bhsd', w.astype(q.dtype), jnp.stack(outs, axis=0))


def hybrid_attention_block(x, positions, norm_g, w_in, w_out):
    bsz, s, _ = x.shape
    h = rms_norm(x, norm_g)
    proj = (h @ w_in).reshape(bsz, s, 3, N_HEADS_A + N_HEADS_B, HEAD_DIM)
    proj = proj.transpose(2, 0, 3, 1, 4)
    q = partial_rotary(proj[0], positions)
    k = partial_rotary(proj[1], positions)
    v = proj[2]
    oa = moba_attention(q[:, :N_HEADS_A], k[:, :N_HEADS_A], v[:, :N_HEADS_A])
    ob = dilated_mixture(q[:, N_HEADS_A:], k[:, N_HEADS_A:], v[:, N_HEADS_A:])
    o = jnp.concatenate([oa, ob], axis=1).transpose(0, 2, 1, 3).reshape(bsz, s, N_HEADS_OUT * HEAD_DIM)
    return x + o @ w_out


def spatial_gating_block(x, norm_g, w_in, b_in, v_gain, v_bias, w_s, b_s, w_out):
    bsz, s, _ = x.shape
    nc = s // GMLP_CHUNK
    h = rms_norm(x, norm_g)
    z = jax.nn.gelu(h @ w_in + b_in, approximate=False)
    u, v = jnp.split(z, 2, axis=-1)
    v = layer_norm(v, v_gain, v_bias).reshape(bsz, nc, GMLP_CHUNK, GMLP_GROUPS, GMLP_GROUP_DIM)
    w_causal = jnp.tril(w_s)
    f = jnp.einsum('gts,bcsgd->bctgd', w_causal, v) + b_s.T[None, None, :, :, None]
    out = u * f.reshape(bsz, s, GMLP_WIDTH)
    return x + out @ w_out


def conv_ffn_block(x, norm_g, w_up, conv_w, conv_b, w_down):
    s = x.shape[1]
    h = rms_norm(x, norm_g)
    a = h @ w_up
    ap = jnp.pad(a, ((0, 0), (CONV_WIDTH - 1, 0), (0, 0)))
    c = conv_b + ap[:, 0:s] * conv_w[0]
    for j in range(1, CONV_WIDTH):
        c = c + ap[:, j:j + s] * conv_w[j]
    gate, up = jnp.split(c, 2, axis=-1)
    return x + (jax.nn.silu(gate) * up) @ w_down


def setup_inputs(seed: int = 0) -> dict:
    key = jax.random.key(seed)
    ks = jax.random.split(key, 24)

    def nrm(k, shape, scale):
        return jax.random.normal(k, shape, F32) * scale

    d, e, f2 = D_MODEL, GMLP_WIDTH, 2 * D_FF
    return {
        "x": nrm(ks[0], (BATCH, SEQ, d), 1.0),
        "positions": jnp.broadcast_to(jnp.arange(SEQ, dtype=jnp.int32), (BATCH, SEQ)),
        "attn_norm": 1.0 + nrm(ks[1], (N_EVEN, d), 0.02),
        "attn_w_in": nrm(ks[2], (N_EVEN, d, QKV_COLS), d ** -0.5),
        "attn_w_out": nrm(ks[3], (N_EVEN, N_HEADS_OUT * HEAD_DIM, d), (N_HEADS_OUT * HEAD_DIM) ** -0.5),
        "sg_norm": 1.0 + nrm(ks[4], (N_ODD, d), 0.02),
        "sg_w_in": nrm(ks[5], (N_ODD, d, 2 * e), d ** -0.5),
        "sg_b_in": nrm(ks[6], (N_ODD, 2 * e), 0.02),
        "sg_v_gain": 1.0 + nrm(ks[7], (N_ODD, e), 0.02),
        "sg_v_bias": nrm(ks[8], (N_ODD, e), 0.02),
        "sg_w_s": nrm(ks[9], (N_ODD, GMLP_GROUPS, GMLP_CHUNK, GMLP_CHUNK), GMLP_CHUNK ** -0.5),
        "sg_b_s": 1.0 + nrm(ks[10], (N_ODD, GMLP_GROUPS, GMLP_CHUNK), 0.02),
        "sg_w_out": nrm(ks[11], (N_ODD, e, d), e ** -0.5),
        "ffn_norm": 1.0 + nrm(ks[12], (DEPTH, d), 0.02),
        "ffn_w_up": nrm(ks[13], (DEPTH, d, f2), d ** -0.5),
        "ffn_conv_w": nrm(ks[14], (DEPTH, CONV_WIDTH, f2), CONV_WIDTH ** -0.5),
        "ffn_conv_b": nrm(ks[15], (DEPTH, f2), 0.02),
        "ffn_w_down": nrm(ks[16], (DEPTH, D_FF, d), D_FF ** -0.5),
        "final_norm": 1.0 + nrm(ks[17], (d,), 0.02),
    }


def reference(x, positions, attn_norm, attn_w_in, attn_w_out, sg_norm, sg_w_in, sg_b_in,
              sg_v_gain, sg_v_bias, sg_w_s, sg_b_s, sg_w_out, ffn_norm, ffn_w_up,
              ffn_conv_w, ffn_conv_b, ffn_w_down, final_norm):
    h = x
    for layer in range(DEPTH):
        i = layer // 2
        if layer % 2 == 0:
            h = hybrid_attention_block(h, positions, attn_norm[i], attn_w_in[i], attn_w_out[i])
        else:
            h = spatial_gating_block(h, sg_norm[i], sg_w_in[i], sg_b_in[i], sg_v_gain[i],
                                     sg_v_bias[i], sg_w_s[i], sg_b_s[i], sg_w_out[i])
        h = conv_ffn_block(h, ffn_norm[layer], ffn_w_up[layer], ffn_conv_w[layer],
                           ffn_conv_b[layer], ffn_w_down[layer])
    return rms_norm(h, final_norm)
```

```python
import functools

import jax
import jax.numpy as jnp
from jax import lax
from jax.experimental import pallas as pl
from jax.experimental.pallas import tpu as pltpu

F32 = jnp.float32
BF16 = jnp.bfloat16

HEAD_DIM = 128
ROT_DIM = HEAD_DIM // 4
ROPE_THETA = 500000.0
N_HEADS_A = 24
N_HEADS_B_GROUP = 8
B_PATTERNS = ((128, 1), (512, 4), (2048, 16))
N_HEADS_QKV = N_HEADS_A + N_HEADS_B_GROUP * len(B_PATTERNS)
MOBA_BLOCK = 256
MOBA_TOPK = 3
GMLP_CHUNK = 128
GMLP_GROUP_DIM = 128
CONV_WIDTH = 3
NORM_EPS = 1e-5

LANES = 128
SUBLANES = 8
VMEM_LIMIT_BYTES = 56 * 1024 * 1024

MASK_VALUE = -1e30


def _params(semantics):
    return pltpu.CompilerParams(dimension_semantics=semantics, vmem_limit_bytes=VMEM_LIMIT_BYTES)


def _dot(a, b):
    return jnp.dot(a, b, preferred_element_type=F32)


def _dot_nt(a, b):
    return lax.dot_general(a, b, (((1,), (1,)), ((), ())), preferred_element_type=F32)


def _dot_tn(a, b):
    return lax.dot_general(a, b, (((0,), (0,)), ((), ())), preferred_element_type=F32)


def _rmsnorm_kernel(x_ref, g_ref, o_ref):
    x = x_ref[...]
    y = x * lax.rsqrt(jnp.mean(x * x, axis=-1, keepdims=True) + NORM_EPS)
    o_ref[...] = (y * g_ref[...]).astype(o_ref.dtype)


def rmsnorm(x, g, out_dtype, rows=256):
    t, d = x.shape
    rows = min(rows, t)
    return pl.pallas_call(
        _rmsnorm_kernel,
        out_shape=jax.ShapeDtypeStruct((t, d), out_dtype),
        grid=(t // rows,),
        in_specs=[pl.BlockSpec((rows, d), lambda i: (i, 0)),
                  pl.BlockSpec((1, d), lambda i: (0, 0))],
        out_specs=pl.BlockSpec((rows, d), lambda i: (i, 0)),
        compiler_params=_params(("parallel",)),
        name="rmsnorm",
    )(x, g.reshape(1, d))


def _qkv_kernel(a_ref, w_ref, cos_ref, sin_lo_ref, sin_hi_ref, o_ref, *, n_rot_tiles):
    j = pl.program_id(1)
    tn = o_ref.shape[1]

    @pl.when(j < n_rot_tiles)
    def _():
        cos = cos_ref[...]
        sin_lo = sin_lo_ref[...]
        sin_hi = sin_hi_ref[...]
        for c in range(tn // HEAD_DIM):
            sl = slice(c * HEAD_DIM, (c + 1) * HEAD_DIM)
            x = _dot(a_ref[...], w_ref[:, sl])
            up = pltpu.roll(x, HEAD_DIM - ROT_DIM // 2, axis=1)
            down = pltpu.roll(x, ROT_DIM // 2, axis=1)
            o_ref[:, sl] = (x * cos + up * sin_lo + down * sin_hi).astype(o_ref.dtype)

    @pl.when(j >= n_rot_tiles)
    def _():
        o_ref[...] = _dot(a_ref[...], w_ref[...]).astype(o_ref.dtype)


def qkv_projection(h, w, cos, sin_lo, sin_hi, n_rot_cols, tm=1024, tn=1024):
    t, k = h.shape
    n = w.shape[1]
    tm, tn = min(tm, t), min(tn, n)
    assert n_rot_cols % tn == 0
    return pl.pallas_call(
        functools.partial(_qkv_kernel, n_rot_tiles=n_rot_cols // tn),
        out_shape=jax.ShapeDtypeStruct((t, n), BF16),
        grid=(t // tm, n // tn),
        in_specs=[pl.BlockSpec((tm, k), lambda i, j: (i, 0)),
                  pl.BlockSpec((k, tn), lambda i, j: (0, j)),
                  pl.BlockSpec((tm, HEAD_DIM), lambda i, j: (i, 0)),
                  pl.BlockSpec((tm, HEAD_DIM), lambda i, j: (i, 0)),
                  pl.BlockSpec((tm, HEAD_DIM), lambda i, j: (i, 0))],
        out_specs=pl.BlockSpec((tm, tn), lambda i, j: (i, j)),
        compiler_params=_params(("parallel", "arbitrary")),
        name="qkv_rotary",
    )(h, w, cos, sin_lo, sin_hi)


def rotary_tables(positions):
    half = ROT_DIM // 2
    inv_freq = jnp.power(ROPE_THETA, -jnp.arange(half, dtype=F32) * (2.0 / ROT_DIM))
    ang = positions.reshape(-1).astype(F32)[:, None] * inv_freq
    cos, sin = jnp.cos(ang), jnp.sin(ang)
    t = ang.shape[0]
    ones = jnp.ones((t, HEAD_DIM - ROT_DIM), F32)
    zeros = jnp.zeros((t, HEAD_DIM - ROT_DIM), F32)
    zh = jnp.zeros((t, half), F32)
    cos_t = jnp.concatenate([cos, cos, ones], axis=1)
    sin_lo = jnp.concatenate([-sin, zh, zeros], axis=1)
    sin_hi = jnp.concatenate([zh, sin, zeros], axis=1)
    return cos_t, sin_lo, sin_hi


def _mm_resid_kernel(a_ref, w_ref, r_ref, o_ref):
    o_ref[...] = r_ref[...] + _dot(a_ref[...], w_ref[...])


def matmul_residual(a, w, resid, tm=1024, tn=1024):
    t, k = a.shape
    n = w.shape[1]
    tm, tn = min(tm, t), min(tn, n)
    return pl.pallas_call(
        _mm_resid_kernel,
        out_shape=jax.ShapeDtypeStruct((t, n), F32),
        grid=(t // tm, n // tn),
        in_specs=[pl.BlockSpec((tm, k), lambda i, j: (i, 0)),
                  pl.BlockSpec((k, tn), lambda i, j: (0, j)),
                  pl.BlockSpec((tm, tn), lambda i, j: (i, j))],
        out_specs=pl.BlockSpec((tm, tn), lambda i, j: (i, j)),
        compiler_params=_params(("parallel", "arbitrary")),
        name="matmul_residual",
    )(a, w, resid)


def _mm_resid_ksplit_kernel(a_ref, w_ref, r_ref, o_ref):
    kk = pl.program_id(2)
    d = _dot(a_ref[...], w_ref[...])

    @pl.when(kk == 0)
    def _():
        o_ref[...] = r_ref[...] + d

    @pl.when(kk > 0)
    def _():
        o_ref[...] += d


def matmul_residual_ksplit(a, w, resid, tm=1024, tn=1024, tk=2048):
    t, k = a.shape
    n = w.shape[1]
    tm, tn, tk = min(tm, t), min(tn, n), min(tk, k)
    assert k % tk == 0
    return pl.pallas_call(
        _mm_resid_ksplit_kernel,
        out_shape=jax.ShapeDtypeStruct((t, n), F32),
        grid=(t // tm, n // tn, k // tk),
        in_specs=[pl.BlockSpec((tm, tk), lambda i, j, kk: (i, kk)),
                  pl.BlockSpec((tk, tn), lambda i, j, kk: (kk, j)),
                  pl.BlockSpec((tm, tn), lambda i, j, kk: (i, j))],
        out_specs=pl.BlockSpec((tm, tn), lambda i, j, kk: (i, j)),
        compiler_params=_params(("parallel", "parallel", "arbitrary")),
        name="matmul_residual_ksplit",
    )(a, w, resid)


def _mm_bias_gelu_kernel(a_ref, w_ref, b_ref, o_ref):
    z = _dot(a_ref[...], w_ref[...]) + b_ref[...]
    o_ref[...] = (0.5 * z * (1.0 + lax.erf(z * (2.0 ** -0.5)))).astype(o_ref.dtype)


def matmul_bias_gelu(a, w, b, tm=1024, tn=1024):
    t, k = a.shape
    n = w.shape[1]
    tm, tn = min(tm, t), min(tn, n)
    return pl.pallas_call(
        _mm_bias_gelu_kernel,
        out_shape=jax.ShapeDtypeStruct((t, n), BF16),
        grid=(t // tm, n // tn),
        in_specs=[pl.BlockSpec((tm, k), lambda i, j: (i, 0)),
                  pl.BlockSpec((k, tn), lambda i, j: (0, j)),
                  pl.BlockSpec((1, tn), lambda i, j: (0, j))],
        out_specs=pl.BlockSpec((tm, tn), lambda i, j: (i, j)),
        compiler_params=_params(("parallel", "arbitrary")),
        name="matmul_bias_gelu",
    )(a, w, b.reshape(1, n))


def _softmax_init(s, v):
    m = jnp.max(s, axis=1, keepdims=True)
    p = jnp.exp(s - m)
    l = jnp.sum(p, axis=1, keepdims=True)
    return m, l, _dot(p.astype(v.dtype), v)


def _softmax_update(state, s, v):
    m, l, acc = state
    m_new = jnp.maximum(m, jnp.max(s, axis=1, keepdims=True))
    alpha = jnp.exp(m - m_new)
    p = jnp.exp(s - m_new)
    l = alpha * l + jnp.sum(p, axis=1, keepdims=True)
    acc = alpha * acc + _dot(p.astype(v.dtype), v)
    return m_new, l, acc


def _moba_kernel(q_ref, k_ref, v_ref, o_ref, kmean_ref, *, n_blocks):
    i = pl.program_id(2)
    blk = MOBA_BLOCK
    scale = HEAD_DIM ** -0.5

    @pl.when(i == 0)
    def _():
        kf = k_ref[...].astype(F32).reshape(n_blocks, blk, HEAD_DIM)
        kmean_ref[...] = jnp.mean(kf, axis=1)

    q = q_ref[...]

    km = kmean_ref[...]
    km_hi = km.astype(BF16)
    km_lo = (km - km_hi.astype(F32)).astype(BF16)
    gate = _dot_nt(km_hi, q) + _dot_nt(km_lo, q)

    blk_idx = lax.broadcasted_iota(jnp.int32, gate.shape, 0)
    rank = jnp.zeros(gate.shape, jnp.int32)
    for m in range(n_blocks):
        gm = gate[m:m + 1, :]
        beats = (gm > gate) | ((gm == gate) & (m < blk_idx))
        rank = rank + jnp.where(beats & (m < i), 1, 0)
    selected = (blk_idx < i) & (rank < MOBA_TOPK)

    pow2 = jnp.left_shift(1, lax.broadcasted_iota(jnp.int32, (n_blocks, LANES), 0)).astype(BF16)
    bits = _dot_tn(selected.astype(BF16), pow2).astype(jnp.int32)

    row = lax.broadcasted_iota(jnp.int32, (blk, blk), 0)
    col = lax.broadcasted_iota(jnp.int32, (blk, blk), 1)
    own = pl.ds(pl.multiple_of(i * blk, blk), blk)
    s = _dot_nt(q, k_ref[own, :]) * scale
    state = _softmax_init(jnp.where(col <= row, s, MASK_VALUE), v_ref[own, :])

    def body(j, state):
        past = pl.ds(pl.multiple_of(j * blk, blk), blk)
        s = _dot_nt(q, k_ref[past, :]) * scale
        keep = (jnp.right_shift(bits, j) & 1) == 1
        bias = jnp.where(keep, 0.0, MASK_VALUE)
        s = s + jnp.concatenate([bias] * (blk // LANES), axis=1)
        return _softmax_update(state, s, v_ref[past, :])

    m, l, acc = lax.fori_loop(0, i, body, state)
    o_ref[...] = (acc / l).astype(o_ref.dtype)


def moba_attention(qkv, bsz, seq, n_heads, q_col0, k_col0, v_col0, out_col0, out, n_out_heads):
    del out
    nb = seq // MOBA_BLOCK
    return pl.pallas_call(
        functools.partial(_moba_kernel, n_blocks=nb),
        out_shape=jax.ShapeDtypeStruct((bsz * seq, n_out_heads * HEAD_DIM), BF16),
        grid=(bsz, n_heads, nb),
        in_specs=[pl.BlockSpec((MOBA_BLOCK, HEAD_DIM), lambda b, h, i: (b * nb + i, q_col0 + h)),
                  pl.BlockSpec((seq, HEAD_DIM), lambda b, h, i: (b, k_col0 + h)),
                  pl.BlockSpec((seq, HEAD_DIM), lambda b, h, i: (b, v_col0 + h))],
        out_specs=pl.BlockSpec((MOBA_BLOCK, HEAD_DIM), lambda b, h, i: (b * nb + i, out_col0 + h)),
        scratch_shapes=[pltpu.VMEM((nb, HEAD_DIM), F32)],
        compiler_params=_params(("parallel", "parallel", "arbitrary")),
        name="moba_attention",
    )(qkv, qkv, qkv)


DIL_BLOCK = 256


def _dilated_kernel(q0_ref, q1_ref, q2_ref, k0_ref, k1_ref, k2_ref, v0_ref, v1_ref, v2_ref, o_ref):
    i = pl.program_id(2)
    blk = DIL_BLOCK
    scale = HEAD_DIM ** -0.5
    row = lax.broadcasted_iota(jnp.int32, (blk, blk), 0)
    col = lax.broadcasted_iota(jnp.int32, (blk, blk), 1)
    rel = row - col
    q_refs = (q0_ref, q1_ref, q2_ref)
    k_refs = (k0_ref, k1_ref, k2_ref)
    v_refs = (v0_ref, v1_ref, v2_ref)

    def scores(g, j):
        window, dil = B_PATTERNS[g]
        n_back = window // dil
        rows = pl.ds(pl.multiple_of(j * blk, blk), blk)
        s = _dot_nt(q_refs[g][...], k_refs[g][rows, :]) * scale
        dist = (i - j) * blk + rel
        ok = (dist >= 0) & (dist <= n_back * dil) & ((dist & (dil - 1)) == 0)
        return jnp.where(ok, s, MASK_VALUE), v_refs[g][rows, :]

    state = _softmax_init(*scores(0, i))
    for g, (window, dil) in enumerate(B_PATTERNS):
        assert dil & (dil - 1) == 0
        blocks_back = -(-window // blk)
        lo = jnp.maximum(i - blocks_back, 0)
        hi = i if g == 0 else i + 1
        state = lax.fori_loop(lo, hi, lambda j, st, g=g: _softmax_update(st, *scores(g, j)), state)
    m, l, acc = state
    o_ref[...] = (acc / l).astype(o_ref.dtype)


def dilated_attention(qkv, bsz, seq, q_col0, k_col0, v_col0, n_out_heads, out_col0):
    nq = seq // DIL_BLOCK
    hg = N_HEADS_B_GROUP

    def qspec(g):
        return pl.BlockSpec((DIL_BLOCK, HEAD_DIM), lambda b, h, i, g=g: (b * nq + i, q_col0 + g * hg + h))

    def kvspec(c0, g):
        return pl.BlockSpec((seq, HEAD_DIM), lambda b, h, i, g=g, c0=c0: (b, c0 + g * hg + h))

    ng = len(B_PATTERNS)
    return pl.pallas_call(
        _dilated_kernel,
        out_shape=jax.ShapeDtypeStruct((bsz * seq, n_out_heads * HEAD_DIM), BF16),
        grid=(bsz, hg, nq),
        in_specs=([qspec(g) for g in range(ng)] + [kvspec(k_col0, g) for g in range(ng)]
                  + [kvspec(v_col0, g) for g in range(ng)]),
        out_specs=pl.BlockSpec((DIL_BLOCK, HEAD_DIM), lambda b, h, i: (b * nq + i, out_col0 + h)),
        compiler_params=_params(("parallel", "parallel", "arbitrary")),
        name="dilated_attention",
    )(*([qkv] * (3 * ng)))


def _sgu_kernel(u_ref, v_ref, gain_ref, bias_ref, ws_ref, bs_ref, o_ref):
    v = v_ref[...].astype(F32)
    mu = jnp.mean(v, axis=-1, keepdims=True)
    vc = v - mu
    vn = vc * lax.rsqrt(jnp.mean(vc * vc, axis=-1, keepdims=True) + NORM_EPS)
    vn = (vn * gain_ref[...] + bias_ref[...]).astype(BF16)
    c = GMLP_CHUNK
    row = lax.broadcasted_iota(jnp.int32, (c, c), 0)
    col = lax.broadcasted_iota(jnp.int32, (c, c), 1)
    causal = col <= row
    for g in range(ws_ref.shape[0]):
        sl = slice(g * GMLP_GROUP_DIM, (g + 1) * GMLP_GROUP_DIM)
        w = jnp.where(causal, ws_ref[g], 0.0).astype(BF16)
        f = _dot(w, vn[:, sl]) + bs_ref[:, g:g + 1]
        o_ref[:, sl] = (u_ref[:, sl].astype(F32) * f).astype(o_ref.dtype)


def spatial_gating(z, gain, bias, w_s, b_s):
    t, e2 = z.shape
    e = e2 // 2
    groups = w_s.shape[0]
    c = GMLP_CHUNK
    return pl.pallas_call(
        _sgu_kernel,
        out_shape=jax.ShapeDtypeStruct((t, e), BF16),
        grid=(t // c,),
        in_specs=[pl.BlockSpec((c, e), lambda i: (i, 0)),
                  pl.BlockSpec((c, e), lambda i: (i, 1)),
                  pl.BlockSpec((1, e), lambda i: (0, 0)),
                  pl.BlockSpec((1, e), lambda i: (0, 0)),
                  pl.BlockSpec((groups, c, c), lambda i: (0, 0, 0)),
                  pl.BlockSpec((c, groups), lambda i: (0, 0))],
        out_specs=pl.BlockSpec((c, e), lambda i: (i, 0)),
        compiler_params=_params(("parallel",)),
        name="spatial_gating",
    )(z, z, gain.reshape(1, e), bias.reshape(1, e), w_s, b_s.T)


CONV_HALO = SUBLANES


def _ffn_up_kernel(h_ref, wg_ref, wu_ref, cwg_ref, cwu_ref, cbg_ref, cbu_ref, o_ref, ag_ref, au_ref,
                   *, tiles_per_seq, chunk):
    i = pl.program_id(1)
    tm, tn = o_ref.shape
    halo = CONV_HALO

    @pl.when(i % tiles_per_seq == 0)
    def _():
        ag_ref[0:halo, :] = jnp.zeros((halo, tn), F32)
        au_ref[0:halo, :] = jnp.zeros((halo, tn), F32)

    def conv(a_ref, sl, cw_ref, cb_ref):
        c = cb_ref[:, sl]
        for tap in range(CONV_WIDTH):
            back = CONV_WIDTH - 1 - tap
            c = c + a_ref[halo - back:halo - back + tm, sl] * cw_ref[tap:tap + 1, sl]
        return c

    for c0 in range(0, tn, chunk):
        sl = slice(c0, c0 + chunk)
        ag_ref[halo:halo + tm, sl] = _dot(h_ref[...], wg_ref[:, sl])
        au_ref[halo:halo + tm, sl] = _dot(h_ref[...], wu_ref[:, sl])
        gate = conv(ag_ref, sl, cwg_ref, cbg_ref)
        up = conv(au_ref, sl, cwu_ref, cbu_ref)
        o_ref[:, sl] = (jax.nn.silu(gate) * up).astype(o_ref.dtype)

    ag_ref[0:halo, :] = ag_ref[tm:tm + halo, :]
    au_ref[0:halo, :] = au_ref[tm:tm + halo, :]


def ffn_up(h, w_up, conv_w, conv_b, seq, tm=1024, tn=512, chunk=256):
    t, k = h.shape
    f = w_up.shape[1] // 2
    tm, tn = min(tm, seq), min(tn, f)
    chunk = min(chunk, tn)
    nj = f // tn
    assert seq % tm == 0 and f % tn == 0 and tn % chunk == 0
    cb = conv_b.reshape(1, 2 * f)
    return pl.pallas_call(
        functools.partial(_ffn_up_kernel, tiles_per_seq=seq // tm, chunk=chunk),
        out_shape=jax.ShapeDtypeStruct((t, f), BF16),
        grid=(nj, t // tm),
        in_specs=[pl.BlockSpec((tm, k), lambda j, i: (i, 0)),
                  pl.BlockSpec((k, tn), lambda j, i: (0, j)),
                  pl.BlockSpec((k, tn), lambda j, i: (0, nj + j)),
                  pl.BlockSpec((CONV_WIDTH, tn), lambda j, i: (0, j)),
                  pl.BlockSpec((CONV_WIDTH, tn), lambda j, i: (0, nj + j)),
                  pl.BlockSpec((1, tn), lambda j, i: (0, j)),
                  pl.BlockSpec((1, tn), lambda j, i: (0, nj + j))],
        out_specs=pl.BlockSpec((tm, tn), lambda j, i: (i, j)),
        scratch_shapes=[pltpu.VMEM((CONV_HALO + tm, tn), F32), pltpu.VMEM((CONV_HALO + tm, tn), F32)],
        compiler_params=_params(("parallel", "arbitrary")),
        name="ffn_up_conv_act",
    )(h, w_up, w_up, conv_w, conv_w, cb, cb)


def conv_ffn_block(x, norm_g, w_up, conv_w, conv_b, w_down, seq):
    h = rmsnorm(x, norm_g, BF16)
    act = ffn_up(h, w_up.astype(BF16), conv_w, conv_b, seq)
    return matmul_residual_ksplit(act, w_down.astype(BF16), x)


def hybrid_attention_block(x, positions, norm_g, w_in, w_out, bsz, seq):
    h = rmsnorm(x, norm_g, BF16)
    cos, sin_lo, sin_hi = rotary_tables(positions)
    nh = N_HEADS_QKV
    qkv = qkv_projection(h, w_in.astype(BF16), cos, sin_lo, sin_hi, n_rot_cols=2 * nh * HEAD_DIM)
    n_out = N_HEADS_A + N_HEADS_B_GROUP
    oa = moba_attention(qkv, bsz, seq, N_HEADS_A, 0, nh, 2 * nh, 0, None, N_HEADS_A)
    ob = dilated_attention(qkv, bsz, seq, N_HEADS_A, nh + N_HEADS_A, 2 * nh + N_HEADS_A,
                           N_HEADS_B_GROUP, 0)
    o = jnp.concatenate([oa, ob], axis=1)
    del n_out
    return matmul_residual(o, w_out.astype(BF16), x)


def spatial_gating_block(x, norm_g, w_in, b_in, v_gain, v_bias, w_s, b_s, w_out):
    h = rmsnorm(x, norm_g, BF16)
    z = matmul_bias_gelu(h, w_in.astype(BF16), b_in)
    gated = spatial_gating(z, v_gain, v_bias, w_s, b_s)
    return matmul_residual(gated, w_out.astype(BF16), x)


def kernel(x, positions, attn_norm, attn_w_in, attn_w_out, sg_norm, sg_w_in, sg_b_in, sg_v_gain,
           sg_v_bias, sg_w_s, sg_b_s, sg_w_out, ffn_norm, ffn_w_up, ffn_conv_w, ffn_conv_b,
           ffn_w_down, final_norm):
    bsz, seq, d = x.shape
    depth = ffn_norm.shape[0]
    h = x.reshape(bsz * seq, d)
    for layer in range(depth):
        i = layer // 2
        if layer % 2 == 0:
            h = hybrid_attention_block(h, positions, attn_norm[i], attn_w_in[i], attn_w_out[i], bsz, seq)
        else:
            h = spatial_gating_block(h, sg_norm[i], sg_w_in[i], sg_b_in[i], sg_v_gain[i], sg_v_bias[i],
                                     sg_w_s[i], sg_b_s[i], sg_w_out[i])
        h = conv_ffn_block(h, ffn_norm[layer], ffn_w_up[layer], ffn_conv_w[layer], ffn_conv_b[layer],
                           ffn_w_down[layer], seq)
    return rmsnorm(h, final_norm, x.dtype).reshape(bsz, seq, d)
```

```python
import functools
import math

import jax
import jax.numpy as jnp
from jax import lax
from jax.experimental import pallas as pl
from jax.experimental.pallas import tpu as pltpu

F32 = jnp.float32
BF16 = jnp.bfloat16

HEAD_DIM = 128
ROT_DIM = HEAD_DIM // 4
ROPE_THETA = 500000.0
N_HEADS_A = 24
N_HEADS_B_GROUP = 8
B_PATTERNS = ((128, 1), (512, 4), (2048, 16))
N_HEADS_QKV = N_HEADS_A + N_HEADS_B_GROUP * len(B_PATTERNS)
MOBA_BLOCK = 256
MOBA_TOPK = 3
GMLP_CHUNK = 128
GMLP_GROUP_DIM = 128
CONV_WIDTH = 3
NORM_EPS = 1e-5

LANES = 128
SUBLANES = 8
MXU_COLS = 256
VMEM_LIMIT_BYTES = 56 * 1024 * 1024

MASK_VALUE = -1e30
Q_SCALE = HEAD_DIM ** -0.5 * math.log2(math.e)


def _params(semantics):
    return pltpu.CompilerParams(dimension_semantics=semantics, vmem_limit_bytes=VMEM_LIMIT_BYTES)


def _dot(a, b):
    return jnp.dot(a, b, preferred_element_type=F32)


def _dot_nt(a, b):
    return lax.dot_general(a, b, (((1,), (1,)), ((), ())), preferred_element_type=F32)


def _dot_tn(a, b):
    return lax.dot_general(a, b, (((0,), (0,)), ((), ())), preferred_element_type=F32)


def _head_slice(h):
    return slice(h * HEAD_DIM, (h + 1) * HEAD_DIM)


def _rmsnorm_kernel(x_ref, g_ref, o_ref):
    x = x_ref[...]
    y = x * lax.rsqrt(jnp.mean(x * x, axis=-1, keepdims=True) + NORM_EPS)
    o_ref[...] = (y * g_ref[...]).astype(o_ref.dtype)


def rmsnorm(x, g, out_dtype, rows=256):
    t, d = x.shape
    rows = min(rows, t)
    return pl.pallas_call(
        _rmsnorm_kernel,
        out_shape=jax.ShapeDtypeStruct((t, d), out_dtype),
        grid=(t // rows,),
        in_specs=[pl.BlockSpec((rows, d), lambda i: (i, 0)),
                  pl.BlockSpec((1, d), lambda i: (0, 0))],
        out_specs=pl.BlockSpec((rows, d), lambda i: (i, 0)),
        compiler_params=_params(("parallel",)),
        name="rmsnorm",
    )(x, g.reshape(1, d))


def _qkv_kernel(a_ref, w_ref, cos_ref, sin_lo_ref, sin_hi_ref, o_ref, *, n_q_tiles, n_rot_tiles):
    j = pl.program_id(1)
    tn = o_ref.shape[1]

    @pl.when(j < n_rot_tiles)
    def _():
        sc = jnp.where(j < n_q_tiles, Q_SCALE, 1.0).astype(F32)
        cos = cos_ref[...] * sc
        sin_lo = sin_lo_ref[...] * sc
        sin_hi = sin_hi_ref[...] * sc
        for c0 in range(0, tn, MXU_COLS):
            x2 = _dot(a_ref[...], w_ref[:, c0:c0 + MXU_COLS])
            for h in range(MXU_COLS // HEAD_DIM):
                x = x2[:, _head_slice(h)]
                up = pltpu.roll(x, HEAD_DIM - ROT_DIM // 2, axis=1)
                down = pltpu.roll(x, ROT_DIM // 2, axis=1)
                lo = c0 + h * HEAD_DIM
                o_ref[:, lo:lo + HEAD_DIM] = (x * cos + up * sin_lo + down * sin_hi).astype(o_ref.dtype)

    @pl.when(j >= n_rot_tiles)
    def _():
        o_ref[...] = _dot(a_ref[...], w_ref[...]).astype(o_ref.dtype)


def qkv_projection(h, w, layer, cos, sin_lo, sin_hi, n_q_cols, n_rot_cols, tm=1024, tn=1024):
    t, k = h.shape
    n = w.shape[2]
    tm, tn = min(tm, t), min(tn, n)
    assert n_q_cols % tn == 0 and n_rot_cols % tn == 0 and tn % MXU_COLS == 0
    return pl.pallas_call(
        functools.partial(_qkv_kernel, n_q_tiles=n_q_cols // tn, n_rot_tiles=n_rot_cols // tn),
        out_shape=jax.ShapeDtypeStruct((t, n), BF16),
        grid=(t // tm, n // tn),
        in_specs=[pl.BlockSpec((tm, k), lambda i, j: (i, 0)),
                  pl.BlockSpec((None, k, tn), lambda i, j: (layer, 0, j)),
                  pl.BlockSpec((tm, HEAD_DIM), lambda i, j: (i, 0)),
                  pl.BlockSpec((tm, HEAD_DIM), lambda i, j: (i, 0)),
                  pl.BlockSpec((tm, HEAD_DIM), lambda i, j: (i, 0))],
        out_specs=pl.BlockSpec((tm, tn), lambda i, j: (i, j)),
        compiler_params=_params(("parallel", "arbitrary")),
        name="qkv_rotary",
    )(h, w, cos, sin_lo, sin_hi)


def rotary_tables(positions):
    half = ROT_DIM // 2
    inv_freq = jnp.power(ROPE_THETA, -jnp.arange(half, dtype=F32) * (2.0 / ROT_DIM))
    ang = positions.reshape(-1).astype(F32)[:, None] * inv_freq
    cos, sin = jnp.cos(ang), jnp.sin(ang)
    t = ang.shape[0]
    ones = jnp.ones((t, HEAD_DIM - ROT_DIM), F32)
    zeros = jnp.zeros((t, HEAD_DIM - ROT_DIM), F32)
    zh = jnp.zeros((t, half), F32)
    cos_t = jnp.concatenate([cos, cos, ones], axis=1)
    sin_lo = jnp.concatenate([-sin, zh, zeros], axis=1)
    sin_hi = jnp.concatenate([zh, sin, zeros], axis=1)
    return cos_t, sin_lo, sin_hi


def _mm_resid_kernel(a_ref, w_ref, r_ref, o_ref):
    o_ref[...] = r_ref[...] + _dot(a_ref[...], w_ref[...])


def matmul_residual(a, w, layer, resid, tm, tn):
    t, k = a.shape
    n = w.shape[2]
    tm, tn = min(tm, t), min(tn, n)
    return pl.pallas_call(
        _mm_resid_kernel,
        out_shape=jax.ShapeDtypeStruct((t, n), F32),
        grid=(t // tm, n // tn),
        in_specs=[pl.BlockSpec((tm, k), lambda i, j: (i, 0)),
                  pl.BlockSpec((None, k, tn), lambda i, j: (layer, 0, j)),
                  pl.BlockSpec((tm, tn), lambda i, j: (i, j))],
        out_specs=pl.BlockSpec((tm, tn), lambda i, j: (i, j)),
        compiler_params=_params(("parallel", "arbitrary")),
        name="matmul_residual",
    )(a, w, resid)


def _mm_bias_gelu_kernel(a_ref, w_ref, b_ref, o_ref):
    z = _dot(a_ref[...], w_ref[...]) + b_ref[...]
    o_ref[...] = (0.5 * z * (1.0 + lax.erf(z * (2.0 ** -0.5)))).astype(o_ref.dtype)


def matmul_bias_gelu(a, w, layer, b, tm=1024, tn=1024):
    t, k = a.shape
    n = w.shape[2]
    tm, tn = min(tm, t), min(tn, n)
    return pl.pallas_call(
        _mm_bias_gelu_kernel,
        out_shape=jax.ShapeDtypeStruct((t, n), BF16),
        grid=(t // tm, n // tn),
        in_specs=[pl.BlockSpec((tm, k), lambda i, j: (i, 0)),
                  pl.BlockSpec((None, k, tn), lambda i, j: (layer, 0, j)),
                  pl.BlockSpec((1, tn), lambda i, j: (0, j))],
        out_specs=pl.BlockSpec((tm, tn), lambda i, j: (i, j)),
        compiler_params=_params(("parallel", "arbitrary")),
        name="matmul_bias_gelu",
    )(a, w, b.reshape(1, n))


def _ones_augmented(v):
    return jnp.concatenate([v, jnp.ones_like(v)], axis=1)


def _softmax_init(s, vaug):
    m = jnp.max(s, axis=1, keepdims=True)
    p = jnp.exp2(s - m)
    return m, _dot(p.astype(vaug.dtype), vaug)


def _softmax_update(state, s, vaug):
    m, accl = state
    m_new = jnp.maximum(m, jnp.max(s, axis=1, keepdims=True))
    alpha = jnp.exp2(m - m_new)
    p = jnp.exp2(s - m_new)
    return m_new, alpha * accl + _dot(p.astype(vaug.dtype), vaug)


def _softmax_finish(state):
    _, accl = state
    return accl[:, :HEAD_DIM] / accl[:, HEAD_DIM:]


MOBA_HEADS_PER_STEP = 4
MOBA_BLOCKS_PER_ITER = 2


def _moba_kernel(q_ref, k_ref, v_ref, o_ref, kmean_ref, vaug_ref, *, n_blocks, heads):
    i = pl.program_id(2)
    blk = MOBA_BLOCK
    per_iter = MOBA_BLOCKS_PER_ITER

    @pl.when(i == 0)
    def _():
        for h in range(heads):
            hs = _head_slice(h)
            kf = k_ref[:, hs].astype(F32).reshape(n_blocks, blk, HEAD_DIM)
            kmean_ref[h] = jnp.mean(kf, axis=1)
            vaug_ref[h] = _ones_augmented(v_ref[:, hs])

    row = lax.broadcasted_iota(jnp.int32, (blk, blk), 0)
    col = lax.broadcasted_iota(jnp.int32, (blk, blk), 1)
    causal = col <= row
    blk_idx = lax.broadcasted_iota(jnp.int32, (n_blocks, blk), 0)
    pow2 = jnp.left_shift(1, lax.broadcasted_iota(jnp.int32, (n_blocks, LANES), 0)).astype(BF16)
    own = pl.ds(pl.multiple_of(i * blk, blk), blk)

    def select_bits(h):
        q = q_ref[:, _head_slice(h)]
        km = kmean_ref[h]
        km_hi = km.astype(BF16)
        km_lo = (km - km_hi.astype(F32)).astype(BF16)
        gate = _dot_nt(km_hi, q) + _dot_nt(km_lo, q)
        rank = jnp.zeros(gate.shape, jnp.int32)
        for m in range(n_blocks):
            gm = gate[m:m + 1, :]
            beats = (gm > gate) | ((gm == gate) & (m < blk_idx))
            rank = rank + jnp.where(beats & (m < i), 1, 0)
        selected = (blk_idx < i) & (rank < MOBA_TOPK)
        return _dot_tn(selected.astype(BF16), pow2).astype(jnp.int32)

    bits = [select_bits(h) for h in range(heads)]

    states = []
    for h in range(heads):
        hs = _head_slice(h)
        s = jnp.where(causal, _dot_nt(q_ref[:, hs], k_ref[own, hs]), MASK_VALUE)
        states.append(_softmax_init(s, vaug_ref[h, own, :]))

    def body(c, states):
        rows = pl.ds(pl.multiple_of(c * (per_iter * blk), per_iter * blk), per_iter * blk)
        new_states = []
        for h in range(heads):
            hs = _head_slice(h)
            s = _dot_nt(q_ref[:, hs], k_ref[rows, hs])
            bias = []
            for t in range(per_iter):
                keep = (jnp.right_shift(bits[h], c * per_iter + t) & 1) == 1
                bias += [jnp.where(keep, 0.0, MASK_VALUE)] * (blk // LANES)
            s = s + jnp.concatenate(bias, axis=1)
            new_states.append(_softmax_update(states[h], s, vaug_ref[h, rows, :]))
        return tuple(new_states)

    n_iters = (i + per_iter - 1) // per_iter
    states = lax.fori_loop(0, n_iters, body, tuple(states))
    for h in range(heads):
        o_ref[:, _head_slice(h)] = _softmax_finish(states[h]).astype(o_ref.dtype)


def moba_attention(qkv, bsz, seq, n_heads, q_head0, k_head0, v_head0):
    nb = seq // MOBA_BLOCK
    hp = MOBA_HEADS_PER_STEP
    assert n_heads % hp == 0 and q_head0 % hp == 0 and k_head0 % hp == 0 and v_head0 % hp == 0
    assert nb % MOBA_BLOCKS_PER_ITER == 0
    w = hp * HEAD_DIM
    return pl.pallas_call(
        functools.partial(_moba_kernel, n_blocks=nb, heads=hp),
        out_shape=jax.ShapeDtypeStruct((bsz * seq, n_heads * HEAD_DIM), BF16),
        grid=(bsz, n_heads // hp, nb),
        in_specs=[pl.BlockSpec((MOBA_BLOCK, w), lambda b, h, i: (b * nb + i, q_head0 // hp + h)),
                  pl.BlockSpec((seq, w), lambda b, h, i: (b, k_head0 // hp + h)),
                  pl.BlockSpec((seq, w), lambda b, h, i: (b, v_head0 // hp + h))],
        out_specs=pl.BlockSpec((MOBA_BLOCK, w), lambda b, h, i: (b * nb + i, h)),
        scratch_shapes=[pltpu.VMEM((hp, nb, HEAD_DIM), F32),
                        pltpu.VMEM((hp, seq, 2 * HEAD_DIM), BF16)],
        compiler_params=_params(("parallel", "parallel", "arbitrary")),
        name="moba_attention",
    )(qkv, qkv, qkv)


DIL_BLOCK = 256
DIL_HEADS_PER_STEP = 2


def _dilated_kernel(q0_ref, q1_ref, q2_ref, k0_ref, k1_ref, k2_ref, v0_ref, v1_ref, v2_ref, o_ref,
                    vaug_ref, *, heads):
    i = pl.program_id(2)
    blk = DIL_BLOCK
    q_refs = (q0_ref, q1_ref, q2_ref)
    k_refs = (k0_ref, k1_ref, k2_ref)
    v_refs = (v0_ref, v1_ref, v2_ref)

    @pl.when(i == 0)
    def _():
        for g in range(len(B_PATTERNS)):
            for h in range(heads):
                vaug_ref[g, h] = _ones_augmented(v_refs[g][:, _head_slice(h)])

    row = lax.broadcasted_iota(jnp.int32, (blk, blk), 0)
    col = lax.broadcasted_iota(jnp.int32, (blk, blk), 1)
    rel = row - col

    def scores(g, h, j):
        window, dil = B_PATTERNS[g]
        n_back = window // dil
        rows = pl.ds(pl.multiple_of(j * blk, blk), blk)
        hs = _head_slice(h)
        s = _dot_nt(q_refs[g][:, hs], k_refs[g][rows, hs])
        dist = (i - j) * blk + rel
        ok = (dist >= 0) & (dist <= n_back * dil) & ((dist & (dil - 1)) == 0)
        return jnp.where(ok, s, MASK_VALUE), vaug_ref[g, h, rows, :]

    states = tuple(_softmax_init(*scores(0, h, i)) for h in range(heads))
    for g, (window, dil) in enumerate(B_PATTERNS):
        assert dil & (dil - 1) == 0
        blocks_back = -(-window // blk)
        lo = jnp.maximum(i - blocks_back, 0)
        hi = i if g == 0 else i + 1

        def body(j, states, g=g):
            return tuple(_softmax_update(states[h], *scores(g, h, j)) for h in range(heads))

        states = lax.fori_loop(lo, hi, body, states)
    for h in range(heads):
        o_ref[:, _head_slice(h)] = _softmax_finish(states[h]).astype(o_ref.dtype)


def dilated_attention(qkv, bsz, seq, q_head0, k_head0, v_head0):
    nq = seq // DIL_BLOCK
    hg = N_HEADS_B_GROUP
    hp = DIL_HEADS_PER_STEP
    ng = len(B_PATTERNS)
    assert hg % hp == 0 and q_head0 % hp == 0 and k_head0 % hp == 0 and v_head0 % hp == 0
    w = hp * HEAD_DIM

    def qspec(g):
        return pl.BlockSpec((DIL_BLOCK, w), lambda b, h, i, g=g: (b * nq + i, (q_head0 + g * hg) // hp + h))

    def kvspec(head0, g):
        return pl.BlockSpec((seq, w), lambda b, h, i, g=g: (b, (head0 + g * hg) // hp + h))

    return pl.pallas_call(
        functools.partial(_dilated_kernel, heads=hp),
        out_shape=jax.ShapeDtypeStruct((bsz * seq, hg * HEAD_DIM), BF16),
        grid=(bsz, hg // hp, nq),
        in_specs=([qspec(g) for g in range(ng)] + [kvspec(k_head0, g) for g in range(ng)]
                  + [kvspec(v_head0, g) for g in range(ng)]),
        out_specs=pl.BlockSpec((DIL_BLOCK, w), lambda b, h, i: (b * nq + i, h)),
        scratch_shapes=[pltpu.VMEM((ng, hp, seq, 2 * HEAD_DIM), BF16)],
        compiler_params=_params(("parallel", "parallel", "arbitrary")),
        name="dilated_attention",
    )(*([qkv] * (3 * ng)))


def _sgu_kernel(u_ref, v_ref, gain_ref, bias_ref, ws_ref, bs_ref, o_ref):
    v = v_ref[...].astype(F32)
    mu = jnp.mean(v, axis=-1, keepdims=True)
    vc = v - mu
    vn = vc * lax.rsqrt(jnp.mean(vc * vc, axis=-1, keepdims=True) + NORM_EPS)
    vn = (vn * gain_ref[...] + bias_ref[...]).astype(BF16)
    c = GMLP_CHUNK
    row = lax.broadcasted_iota(jnp.int32, (c, c), 0)
    col = lax.broadcasted_iota(jnp.int32, (c, c), 1)
    causal = col <= row
    for g in range(ws_ref.shape[0]):
        sl = slice(g * GMLP_GROUP_DIM, (g + 1) * GMLP_GROUP_DIM)
        w = jnp.where(causal, ws_ref[g], 0.0).astype(BF16)
        f = _dot(w, vn[:, sl]) + bs_ref[:, g:g + 1]
        o_ref[:, sl] = (u_ref[:, sl].astype(F32) * f).astype(o_ref.dtype)


def spatial_gating(z, gain, bias, w_s, b_s):
    t, e2 = z.shape
    e = e2 // 2
    groups = w_s.shape[0]
    c = GMLP_CHUNK
    return pl.pallas_call(
        _sgu_kernel,
        out_shape=jax.ShapeDtypeStruct((t, e), BF16),
        grid=(t // c,),
        in_specs=[pl.BlockSpec((c, e), lambda i: (i, 0)),
                  pl.BlockSpec((c, e), lambda i: (i, 1)),
                  pl.BlockSpec((1, e), lambda i: (0, 0)),
                  pl.BlockSpec((1, e), lambda i: (0, 0)),
                  pl.BlockSpec((groups, c, c), lambda i: (0, 0, 0)),
                  pl.BlockSpec((c, groups), lambda i: (0, 0))],
        out_specs=pl.BlockSpec((c, e), lambda i: (i, 0)),
        compiler_params=_params(("parallel",)),
        name="spatial_gating",
    )(z, z, gain.reshape(1, e), bias.reshape(1, e), w_s, b_s.T)


def _causal_conv(a, prev, cw, cb):
    body = cb + a * cw[CONV_WIDTH - 1:CONV_WIDTH]
    first = a[0:SUBLANES]
    ext = jnp.concatenate([prev, first], axis=0)
    head = cb + first * cw[CONV_WIDTH - 1:CONV_WIDTH]
    for back in range(1, CONV_WIDTH):
        tap = cw[CONV_WIDTH - 1 - back:CONV_WIDTH - back]
        body = body + pltpu.roll(a, back, axis=0) * tap
        head = head + pltpu.roll(ext, back, axis=0)[SUBLANES:] * tap
    return body, head


def _ffn_up_kernel(h_ref, wg_ref, wu_ref, cwg_ref, cwu_ref, cbg_ref, cbu_ref, o_ref, pg_ref, pu_ref,
                   *, tiles_per_seq, chunk):
    i = pl.program_id(1)
    tm, tn = o_ref.shape

    @pl.when(i % tiles_per_seq == 0)
    def _():
        pg_ref[...] = jnp.zeros(pg_ref.shape, F32)
        pu_ref[...] = jnp.zeros(pu_ref.shape, F32)

    for c0 in range(0, tn, chunk):
        sl = slice(c0, c0 + chunk)
        ag = _dot(h_ref[...], wg_ref[:, sl])
        au = _dot(h_ref[...], wu_ref[:, sl])
        gate, gate_head = _causal_conv(ag, pg_ref[:, sl], cwg_ref[:, sl], cbg_ref[:, sl])
        up, up_head = _causal_conv(au, pu_ref[:, sl], cwu_ref[:, sl], cbu_ref[:, sl])
        o_ref[:, sl] = (jax.nn.silu(gate) * up).astype(o_ref.dtype)
        o_ref[0:SUBLANES, sl] = (jax.nn.silu(gate_head) * up_head).astype(o_ref.dtype)
        pg_ref[:, sl] = ag[tm - SUBLANES:tm]
        pu_ref[:, sl] = au[tm - SUBLANES:tm]


def ffn_up(h, w_up, layer, conv_w, conv_b, seq, tm=1024, tn=512, chunk=MXU_COLS):
    t, k = h.shape
    f = w_up.shape[2] // 2
    tm, tn = min(tm, seq), min(tn, f)
    chunk = min(chunk, tn)
    nj = f // tn
    assert seq % tm == 0 and f % tn == 0 and tn % chunk == 0
    cb = conv_b.reshape(1, 2 * f)
    return pl.pallas_call(
        functools.partial(_ffn_up_kernel, tiles_per_seq=seq // tm, chunk=chunk),
        out_shape=jax.ShapeDtypeStruct((t, f), BF16),
        grid=(nj, t // tm),
        in_specs=[pl.BlockSpec((tm, k), lambda j, i: (i, 0)),
                  pl.BlockSpec((None, k, tn), lambda j, i: (layer, 0, j)),
                  pl.BlockSpec((None, k, tn), lambda j, i: (layer, 0, nj + j)),
                  pl.BlockSpec((CONV_WIDTH, tn), lambda j, i: (0, j)),
                  pl.BlockSpec((CONV_WIDTH, tn), lambda j, i: (0, nj + j)),
                  pl.BlockSpec((1, tn), lambda j, i: (0, j)),
                  pl.BlockSpec((1, tn), lambda j, i: (0, nj + j))],
        out_specs=pl.BlockSpec((tm, tn), lambda j, i: (i, j)),
        scratch_shapes=[pltpu.VMEM((SUBLANES, tn), F32), pltpu.VMEM((SUBLANES, tn), F32)],
        compiler_params=_params(("parallel", "arbitrary")),
        name="ffn_up_conv_act",
    )(h, w_up, w_up, conv_w, conv_w, cb, cb)


def conv_ffn_block(x, norm_g, w_up, w_down, layer, conv_w, conv_b, seq):
    h = rmsnorm(x, norm_g, BF16)
    act = ffn_up(h, w_up, layer, conv_w, conv_b, seq)
    return matmul_residual(act, w_down, layer, x, tm=512, tn=256)


def hybrid_attention_block(x, positions, norm_g, w_in, w_out, layer, bsz, seq):
    h = rmsnorm(x, norm_g, BF16)
    cos, sin_lo, sin_hi = rotary_tables(positions)
    nh = N_HEADS_QKV
    qkv = qkv_projection(h, w_in, layer, cos, sin_lo, sin_hi,
                         n_q_cols=nh * HEAD_DIM, n_rot_cols=2 * nh * HEAD_DIM)
    oa = moba_attention(qkv, bsz, seq, N_HEADS_A, 0, nh, 2 * nh)
    ob = dilated_attention(qkv, bsz, seq, N_HEADS_A, nh + N_HEADS_A, 2 * nh + N_HEADS_A)
    o = jnp.concatenate([oa, ob], axis=1)
    return matmul_residual(o, w_out, layer, x, tm=1024, tn=1024)


def spatial_gating_block(x, norm_g, w_in, w_out, layer, b_in, v_gain, v_bias, w_s, b_s):
    h = rmsnorm(x, norm_g, BF16)
    z = matmul_bias_gelu(h, w_in, layer, b_in)
    gated = spatial_gating(z, v_gain, v_bias, w_s, b_s)
    return matmul_residual(gated, w_out, layer, x, tm=1024, tn=1024)


def kernel(x, positions, attn_norm, attn_w_in, attn_w_out, sg_norm, sg_w_in, sg_b_in, sg_v_gain,
           sg_v_bias, sg_w_s, sg_b_s, sg_w_out, ffn_norm, ffn_w_up, ffn_conv_w, ffn_conv_b,
           ffn_w_down, final_norm):
    bsz, seq, d = x.shape
    depth = ffn_norm.shape[0]
    attn_w_in, attn_w_out, sg_w_in, sg_w_out, ffn_w_up, ffn_w_down = (
        w.astype(BF16) for w in (attn_w_in, attn_w_out, sg_w_in, sg_w_out, ffn_w_up, ffn_w_down))
    h = x.reshape(bsz * seq, d)
    for layer in range(depth):
        i = layer // 2
        if layer % 2 == 0:
            h = hybrid_attention_block(h, positions, attn_norm[i], attn_w_in, attn_w_out, i, bsz, seq)
        else:
            h = spatial_gating_block(h, sg_norm[i], sg_w_in, sg_w_out, i, sg_b_in[i], sg_v_gain[i],
                                     sg_v_bias[i], sg_w_s[i], sg_b_s[i])
        h = conv_ffn_block(h, ffn_norm[layer], ffn_w_up, ffn_w_down, layer, ffn_conv_w[layer],
                           ffn_conv_b[layer], seq)
    return rmsnorm(h, final_norm, x.dtype).reshape(bsz, seq, d)
```

```python
import functools
import math

import jax
import jax.numpy as jnp
from jax import lax
from jax.experimental import pallas as pl
from jax.experimental.pallas import tpu as pltpu

F32 = jnp.float32
BF16 = jnp.bfloat16

HEAD_DIM = 128
ROT_DIM = HEAD_DIM // 4
ROPE_THETA = 500000.0
N_HEADS_A = 24
N_HEADS_B_GROUP = 8
B_PATTERNS = ((128, 1), (512, 4), (2048, 16))
N_HEADS_QKV = N_HEADS_A + N_HEADS_B_GROUP * len(B_PATTERNS)
MOBA_BLOCK = 256
MOBA_TOPK = 3
GMLP_CHUNK = 128
GMLP_GROUP_DIM = 128
CONV_WIDTH = 3
NORM_EPS = 1e-5

LANES = 128
SUBLANES = 8
MXU_COLS = 256
VMEM_LIMIT_BYTES = 56 * 1024 * 1024

MASK_VALUE = -1e30
LOG2_E = math.log2(math.e)
Q_SCALE = HEAD_DIM ** -0.5 * LOG2_E


def _params(semantics):
    return pltpu.CompilerParams(dimension_semantics=semantics, vmem_limit_bytes=VMEM_LIMIT_BYTES)


def _dot(a, b):
    return jnp.dot(a, b, preferred_element_type=F32)


def _dot_nt(a, b):
    return lax.dot_general(a, b, (((1,), (1,)), ((), ())), preferred_element_type=F32)


def _dot_tn(a, b):
    return lax.dot_general(a, b, (((0,), (0,)), ((), ())), preferred_element_type=F32)


def _head_slice(h):
    return slice(h * HEAD_DIM, (h + 1) * HEAD_DIM)


def _rmsnorm_kernel(x_ref, g_ref, o_ref):
    x = x_ref[...]
    y = x * lax.rsqrt(jnp.mean(x * x, axis=-1, keepdims=True) + NORM_EPS)
    o_ref[...] = (y * g_ref[...]).astype(o_ref.dtype)


def rmsnorm(x, g, out_dtype, rows=256):
    t, d = x.shape
    rows = min(rows, t)
    return pl.pallas_call(
        _rmsnorm_kernel,
        out_shape=jax.ShapeDtypeStruct((t, d), out_dtype),
        grid=(t // rows,),
        in_specs=[pl.BlockSpec((rows, d), lambda i: (i, 0)),
                  pl.BlockSpec((1, d), lambda i: (0, 0))],
        out_specs=pl.BlockSpec((rows, d), lambda i: (i, 0)),
        compiler_params=_params(("parallel",)),
        name="rmsnorm",
    )(x, g.reshape(1, d))


def _qkv_kernel(a_ref, w32_ref, cos_ref, sin_lo_ref, sin_hi_ref, o_ref, w_ref, *, n_q_tiles, n_rot_tiles):
    j = pl.program_id(0)
    tn = o_ref.shape[1]

    @pl.when(pl.program_id(1) == 0)
    def _():
        _cast_rows(w32_ref, w_ref)

    @pl.when(j < n_rot_tiles)
    def _():
        sc = jnp.where(j < n_q_tiles, Q_SCALE, 1.0).astype(F32)
        cos = cos_ref[...] * sc
        sin_lo = sin_lo_ref[...] * sc
        sin_hi = sin_hi_ref[...] * sc
        for c0 in range(0, tn, MXU_COLS):
            x2 = _dot(a_ref[...], w_ref[:, c0:c0 + MXU_COLS])
            for h in range(MXU_COLS // HEAD_DIM):
                x = x2[:, _head_slice(h)]
                up = pltpu.roll(x, HEAD_DIM - ROT_DIM // 2, axis=1)
                down = pltpu.roll(x, ROT_DIM // 2, axis=1)
                lo = c0 + h * HEAD_DIM
                o_ref[:, lo:lo + HEAD_DIM] = (x * cos + up * sin_lo + down * sin_hi).astype(o_ref.dtype)

    @pl.when(j >= n_rot_tiles)
    def _():
        o_ref[...] = _dot(a_ref[...], w_ref[...]).astype(o_ref.dtype)


def qkv_projection(h, w, layer, cos, sin_lo, sin_hi, n_q_cols, n_rot_cols, tm=1024, tn=512):
    t, k = h.shape
    n = w.shape[2]
    tm, tn = min(tm, t), min(tn, n)
    assert n_q_cols % tn == 0 and n_rot_cols % tn == 0 and tn % MXU_COLS == 0
    return pl.pallas_call(
        functools.partial(_qkv_kernel, n_q_tiles=n_q_cols // tn, n_rot_tiles=n_rot_cols // tn),
        out_shape=jax.ShapeDtypeStruct((t, n), BF16),
        grid=(n // tn, t // tm),
        in_specs=[pl.BlockSpec((tm, k), lambda j, i: (i, 0)),
                  pl.BlockSpec((None, k, tn), lambda j, i: (layer, 0, j)),
                  pl.BlockSpec((tm, HEAD_DIM), lambda j, i: (i, 0)),
                  pl.BlockSpec((tm, HEAD_DIM), lambda j, i: (i, 0)),
                  pl.BlockSpec((tm, HEAD_DIM), lambda j, i: (i, 0))],
        out_specs=pl.BlockSpec((tm, tn), lambda j, i: (i, j)),
        scratch_shapes=[pltpu.VMEM((k, tn), BF16)],
        compiler_params=_params(("parallel", "arbitrary")),
        name="qkv_rotary",
    )(h, w, cos, sin_lo, sin_hi)


def rotary_tables(positions):
    half = ROT_DIM // 2
    inv_freq = jnp.power(ROPE_THETA, -jnp.arange(half, dtype=F32) * (2.0 / ROT_DIM))
    ang = positions.reshape(-1).astype(F32)[:, None] * inv_freq
    cos, sin = jnp.cos(ang), jnp.sin(ang)
    t = ang.shape[0]
    ones = jnp.ones((t, HEAD_DIM - ROT_DIM), F32)
    zeros = jnp.zeros((t, HEAD_DIM - ROT_DIM), F32)
    zh = jnp.zeros((t, half), F32)
    cos_t = jnp.concatenate([cos, cos, ones], axis=1)
    sin_lo = jnp.concatenate([-sin, zh, zeros], axis=1)
    sin_hi = jnp.concatenate([zh, sin, zeros], axis=1)
    return cos_t, sin_lo, sin_hi


def _mm_resid_kernel(a_ref, w_ref, r_ref, o_ref):
    o_ref[...] = r_ref[...] + _dot(a_ref[...], w_ref[...])


def matmul_residual(a, w, layer, resid, tm, tn):
    t, k = a.shape
    n = w.shape[2]
    tm, tn = min(tm, t), min(tn, n)
    return pl.pallas_call(
        _mm_resid_kernel,
        out_shape=jax.ShapeDtypeStruct((t, n), F32),
        grid=(t // tm, n // tn),
        in_specs=[pl.BlockSpec((tm, k), lambda i, j: (i, 0)),
                  pl.BlockSpec((None, k, tn), lambda i, j: (layer, 0, j)),
                  pl.BlockSpec((tm, tn), lambda i, j: (i, j))],
        out_specs=pl.BlockSpec((tm, tn), lambda i, j: (i, j)),
        compiler_params=_params(("parallel", "arbitrary")),
        name="matmul_residual",
    )(a, w, resid)


def _mm_bias_gelu_kernel(a_ref, w_ref, b_ref, o_ref):
    z = _dot(a_ref[...], w_ref[...]) + b_ref[...]
    o_ref[...] = (0.5 * z * (1.0 + lax.erf(z * (2.0 ** -0.5)))).astype(o_ref.dtype)


def matmul_bias_gelu(a, w, layer, b, tm=1024, tn=1024):
    t, k = a.shape
    n = w.shape[2]
    tm, tn = min(tm, t), min(tn, n)
    return pl.pallas_call(
        _mm_bias_gelu_kernel,
        out_shape=jax.ShapeDtypeStruct((t, n), BF16),
        grid=(t // tm, n // tn),
        in_specs=[pl.BlockSpec((tm, k), lambda i, j: (i, 0)),
                  pl.BlockSpec((None, k, tn), lambda i, j: (layer, 0, j)),
                  pl.BlockSpec((1, tn), lambda i, j: (0, j))],
        out_specs=pl.BlockSpec((tm, tn), lambda i, j: (i, j)),
        compiler_params=_params(("parallel", "arbitrary")),
        name="matmul_bias_gelu",
    )(a, w, b.reshape(1, n))


def _ones_augmented(v):
    return jnp.concatenate([v, jnp.ones_like(v)], axis=1)


def _softmax_init(s, vaug):
    m = jnp.max(s, axis=1, keepdims=True)
    p = jnp.exp2(s - m)
    return m, _dot(p.astype(vaug.dtype), vaug)


def _softmax_update(state, s, vaug):
    m, accl = state
    m_new = jnp.maximum(m, jnp.max(s, axis=1, keepdims=True))
    alpha = jnp.exp2(m - m_new)
    p = jnp.exp2(s - m_new)
    return m_new, alpha * accl + _dot(p.astype(vaug.dtype), vaug)


def _softmax_finish(state):
    _, accl = state
    return accl[:, :HEAD_DIM] / accl[:, HEAD_DIM:]


MOBA_HEADS_PER_STEP = 4
MOBA_BLOCKS_PER_ITER = 2


def _moba_kernel(q_ref, k_ref, v_ref, o_ref, kmean_ref, vaug_ref, *, n_blocks, heads):
    i = pl.program_id(2)
    blk = MOBA_BLOCK
    per_iter = MOBA_BLOCKS_PER_ITER

    @pl.when(i == 0)
    def _():
        for h in range(heads):
            hs = _head_slice(h)
            kf = k_ref[:, hs].astype(F32).reshape(n_blocks, blk, HEAD_DIM)
            kmean_ref[h] = jnp.mean(kf, axis=1)
            vaug_ref[h] = _ones_augmented(v_ref[:, hs])

    row = lax.broadcasted_iota(jnp.int32, (blk, blk), 0)
    col = lax.broadcasted_iota(jnp.int32, (blk, blk), 1)
    causal = col <= row
    blk_idx = lax.broadcasted_iota(jnp.int32, (n_blocks, blk), 0)
    pow2 = jnp.left_shift(1, lax.broadcasted_iota(jnp.int32, (n_blocks, LANES), 0)).astype(BF16)
    own = pl.ds(pl.multiple_of(i * blk, blk), blk)

    def select_bits(h):
        q = q_ref[:, _head_slice(h)]
        km = kmean_ref[h]
        km_hi = km.astype(BF16)
        km_lo = (km - km_hi.astype(F32)).astype(BF16)
        gate = _dot_nt(km_hi, q) + _dot_nt(km_lo, q)
        rank = jnp.zeros(gate.shape, jnp.int32)
        for m in range(n_blocks):
            gm = gate[m:m + 1, :]
            beats = (gm > gate) | ((gm == gate) & (m < blk_idx))
            rank = rank + jnp.where(beats & (m < i), 1, 0)
        selected = (blk_idx < i) & (rank < MOBA_TOPK)
        return _dot_tn(selected.astype(BF16), pow2).astype(jnp.int32)

    bits = [select_bits(h) for h in range(heads)]

    states = []
    for h in range(heads):
        hs = _head_slice(h)
        s = jnp.where(causal, _dot_nt(q_ref[:, hs], k_ref[own, hs]), MASK_VALUE)
        states.append(_softmax_init(s, vaug_ref[h, own, :]))

    def body(c, states):
        rows = pl.ds(pl.multiple_of(c * (per_iter * blk), per_iter * blk), per_iter * blk)
        new_states = []
        for h in range(heads):
            hs = _head_slice(h)
            s = _dot_nt(q_ref[:, hs], k_ref[rows, hs])
            bias = []
            for t in range(per_iter):
                keep = (jnp.right_shift(bits[h], c * per_iter + t) & 1) == 1
                bias += [jnp.where(keep, 0.0, MASK_VALUE)] * (blk // LANES)
            s = s + jnp.concatenate(bias, axis=1)
            new_states.append(_softmax_update(states[h], s, vaug_ref[h, rows, :]))
        return tuple(new_states)

    n_iters = (i + per_iter - 1) // per_iter
    states = lax.fori_loop(0, n_iters, body, tuple(states))
    for h in range(heads):
        o_ref[:, _head_slice(h)] = _softmax_finish(states[h]).astype(o_ref.dtype)


def moba_attention(qkv, bsz, seq, n_heads, q_head0, k_head0, v_head0, n_out_heads):
    nb = seq // MOBA_BLOCK
    hp = MOBA_HEADS_PER_STEP
    assert n_heads % hp == 0 and q_head0 % hp == 0 and k_head0 % hp == 0 and v_head0 % hp == 0
    assert nb % MOBA_BLOCKS_PER_ITER == 0
    w = hp * HEAD_DIM
    return pl.pallas_call(
        functools.partial(_moba_kernel, n_blocks=nb, heads=hp),
        out_shape=jax.ShapeDtypeStruct((bsz * seq, n_out_heads * HEAD_DIM), BF16),
        grid=(bsz, n_heads // hp, nb),
        in_specs=[pl.BlockSpec((MOBA_BLOCK, w), lambda b, h, i: (b * nb + i, q_head0 // hp + h)),
                  pl.BlockSpec((seq, w), lambda b, h, i: (b, k_head0 // hp + h)),
                  pl.BlockSpec((seq, w), lambda b, h, i: (b, v_head0 // hp + h))],
        out_specs=pl.BlockSpec((MOBA_BLOCK, w), lambda b, h, i: (b * nb + i, h)),
        scratch_shapes=[pltpu.VMEM((hp, nb, HEAD_DIM), F32),
                        pltpu.VMEM((hp, seq, 2 * HEAD_DIM), BF16)],
        compiler_params=_params(("parallel", "parallel", "arbitrary")),
        name="moba_attention",
    )(qkv, qkv, qkv)


DIL_TILE = 256
DIL_BACK = 128


def _residue_permutation(dil):
    per = DIL_TILE // dil
    r = lax.broadcasted_iota(jnp.int32, (DIL_TILE, DIL_TILE), 0)
    c = lax.broadcasted_iota(jnp.int32, (DIL_TILE, DIL_TILE), 1)
    return (c == (r % per) * dil + r // per).astype(BF16)


def _band_tile(q, k, vaug, q0, k0):
    n = k.shape[0]
    s = _dot_nt(q, k)
    dist = (q0 - k0) + (lax.broadcasted_iota(jnp.int32, (DIL_TILE, n), 0)
                        - lax.broadcasted_iota(jnp.int32, (DIL_TILE, n), 1))
    s = jnp.where((dist >= 0) & (dist <= DIL_BACK), s, MASK_VALUE)
    m = jnp.max(s, axis=1, keepdims=True)
    p = jnp.exp2(s - m)
    ol = _dot(p.astype(vaug.dtype), vaug)
    l = ol[:, HEAD_DIM:]
    return ol[:, :HEAD_DIM] / l, m + jnp.log(l) * LOG2_E


def _dilated_kernel(q0_ref, q1_ref, q2_ref, k0_ref, k1_ref, k2_ref, v0_ref, v1_ref, v2_ref, _buf_ref,
                    o_ref, v0aug_ref, qs1_ref, ks1_ref, vs1_ref, qs2_ref, ks2_ref, vs2_ref, out_ref, lse_ref, *, seq):
    tile = DIL_TILE
    n_tiles = seq // tile
    keys = tile + DIL_BACK
    groups = ((q1_ref, k1_ref, v1_ref, qs1_ref, ks1_ref, vs1_ref),
              (q2_ref, k2_ref, v2_ref, qs2_ref, ks2_ref, vs2_ref))

    v0aug_ref[...] = _ones_augmented(v0_ref[...])
    for (_, dil), refs in zip(B_PATTERNS[1:], groups):
        per = tile // dil
        perm = _residue_permutation(dil)

        def gather(b, carry, per=per, perm=perm, refs=refs, dil=dil):
            q_ref, k_ref, v_ref, qs_ref, ks_ref, vs_ref = refs
            rows = pl.ds(pl.multiple_of(b * tile, tile), tile)
            dst = pl.ds(pl.multiple_of(b * per, per), per)
            for src, out in ((q_ref, qs_ref), (k_ref, ks_ref)):
                out[:, dst, :] = _dot(perm, src[rows, :]).astype(BF16).reshape(dil, per, HEAD_DIM)
            vp = _dot(perm, v_ref[rows, :]).astype(BF16)
            vs_ref[:, dst, :] = _ones_augmented(vp).reshape(dil, per, 2 * HEAD_DIM)
            return carry

        lax.fori_loop(0, n_tiles, gather, 0)

    def window(t0):
        k0 = jnp.maximum(t0 - DIL_BACK, 0)
        return k0, pl.ds(pl.multiple_of(k0, DIL_BACK), keys)

    def tile_g0(i, carry):
        t0 = pl.multiple_of(i * tile, tile)
        k0, krows = window(t0)
        out, lse = _band_tile(q0_ref[pl.ds(t0, tile), :], k0_ref[krows, :], v0aug_ref[krows, :], t0, k0)
        out_ref[0, pl.ds(t0, tile), :] = out
        lse_ref[0, pl.ds(t0, tile), :] = lse
        return carry

    lax.fori_loop(0, n_tiles, tile_g0, 0, unroll=2)

    for g, ((_, dil), refs) in enumerate(zip(B_PATTERNS[1:], groups), start=1):
        sub_tiles = n_tiles // dil

        def tile_g(it, carry, g=g, dil=dil, sub_tiles=sub_tiles, refs=refs[3:]):
            qs_ref, ks_ref, vs_ref = refs
            res, ti = it // sub_tiles, it % sub_tiles
            t0 = pl.multiple_of(ti * tile, tile)
            if sub_tiles == 1:
                k0, krows = 0, pl.ds(0, tile)
            else:
                k0, krows = window(t0)
            out, lse = _band_tile(qs_ref[res, pl.ds(t0, tile), :], ks_ref[res, krows, :],
                                  vs_ref[res, krows, :], t0, k0)
            natural = pl.ds(t0 * dil + res, tile, stride=dil)
            out_ref[g, natural, :] = out
            lse_ref[g, natural, :] = lse
            return carry

        lax.fori_loop(0, n_tiles, tile_g, 0, unroll=2)

    def merge(i, carry):
        rows = pl.ds(pl.multiple_of(i * tile, tile), tile)
        lses = [lse_ref[g, rows, :] for g in range(len(B_PATTERNS))]
        top = functools.reduce(jnp.maximum, lses)
        ws = [jnp.exp2(lse - top) for lse in lses]
        num = sum(w * out_ref[g, rows, :] for g, w in enumerate(ws))
        o_ref[rows, :] = (num / sum(ws)).astype(o_ref.dtype)
        return carry

    lax.fori_loop(0, n_tiles, merge, 0)


def dilated_attention(qkv, out_buf, bsz, seq, q_head0, k_head0, v_head0, out_head0):
    hg = N_HEADS_B_GROUP
    ng = len(B_PATTERNS)
    tile = DIL_TILE
    assert all(w // d == DIL_BACK and tile % d == 0 and (seq // tile) % d == 0 for w, d in B_PATTERNS)
    assert B_PATTERNS[0][1] == 1 and seq % tile == 0 and seq >= tile + DIL_BACK

    def spec(head0, g):
        return pl.BlockSpec((seq, HEAD_DIM), lambda b, h, g=g: (b, head0 + g * hg + h))

    scratch = [pltpu.VMEM((seq, 2 * HEAD_DIM), BF16)]
    for _, dil in B_PATTERNS[1:]:
        scratch += [pltpu.VMEM((dil, seq // dil, HEAD_DIM), BF16), pltpu.VMEM((dil, seq // dil, HEAD_DIM), BF16),
                    pltpu.VMEM((dil, seq // dil, 2 * HEAD_DIM), BF16)]
    scratch += [pltpu.VMEM((ng, seq, HEAD_DIM), F32), pltpu.VMEM((ng, seq, HEAD_DIM), F32)]
    return pl.pallas_call(
        functools.partial(_dilated_kernel, seq=seq),
        out_shape=jax.ShapeDtypeStruct(out_buf.shape, out_buf.dtype),
        grid=(bsz, hg),
        in_specs=([spec(q_head0, g) for g in range(ng)] + [spec(k_head0, g) for g in range(ng)]
                  + [spec(v_head0, g) for g in range(ng)] + [pl.BlockSpec(memory_space=pl.ANY)]),
        out_specs=pl.BlockSpec((seq, HEAD_DIM), lambda b, h: (b, out_head0 + h)),
        scratch_shapes=scratch,
        input_output_aliases={3 * ng: 0},
        compiler_params=_params(("parallel", "parallel")),
        name="dilated_attention",
    )(*([qkv] * (3 * ng)), out_buf)


def _sgu_kernel(u_ref, v_ref, gain_ref, bias_ref, ws_ref, bs_ref, o_ref):
    v = v_ref[...].astype(F32)
    mu = jnp.mean(v, axis=-1, keepdims=True)
    vc = v - mu
    vn = vc * lax.rsqrt(jnp.mean(vc * vc, axis=-1, keepdims=True) + NORM_EPS)
    vn = (vn * gain_ref[...] + bias_ref[...]).astype(BF16)
    c = GMLP_CHUNK
    row = lax.broadcasted_iota(jnp.int32, (c, c), 0)
    col = lax.broadcasted_iota(jnp.int32, (c, c), 1)
    causal = col <= row
    for g in range(ws_ref.shape[0]):
        sl = slice(g * GMLP_GROUP_DIM, (g + 1) * GMLP_GROUP_DIM)
        w = jnp.where(causal, ws_ref[g], 0.0).astype(BF16)
        f = _dot(w, vn[:, sl]) + bs_ref[:, g:g + 1]
        o_ref[:, sl] = (u_ref[:, sl].astype(F32) * f).astype(o_ref.dtype)


def spatial_gating(z, gain, bias, w_s, b_s):
    t, e2 = z.shape
    e = e2 // 2
    groups = w_s.shape[0]
    c = GMLP_CHUNK
    return pl.pallas_call(
        _sgu_kernel,
        out_shape=jax.ShapeDtypeStruct((t, e), BF16),
        grid=(t // c,),
        in_specs=[pl.BlockSpec((c, e), lambda i: (i, 0)),
                  pl.BlockSpec((c, e), lambda i: (i, 1)),
                  pl.BlockSpec((1, e), lambda i: (0, 0)),
                  pl.BlockSpec((1, e), lambda i: (0, 0)),
                  pl.BlockSpec((groups, c, c), lambda i: (0, 0, 0)),
                  pl.BlockSpec((c, groups), lambda i: (0, 0))],
        out_specs=pl.BlockSpec((c, e), lambda i: (i, 0)),
        compiler_params=_params(("parallel",)),
        name="spatial_gating",
    )(z, z, gain.reshape(1, e), bias.reshape(1, e), w_s, b_s.T)


def _causal_conv(a, prev, cw, cb):
    body = cb + a * cw[CONV_WIDTH - 1:CONV_WIDTH]
    first = a[0:SUBLANES]
    ext = jnp.concatenate([prev, first], axis=0)
    head = cb + first * cw[CONV_WIDTH - 1:CONV_WIDTH]
    for back in range(1, CONV_WIDTH):
        tap = cw[CONV_WIDTH - 1 - back:CONV_WIDTH - back]
        body = body + pltpu.roll(a, back, axis=0) * tap
        head = head + pltpu.roll(ext, back, axis=0)[SUBLANES:] * tap
    return body, head


def _cast_rows(src_ref, dst_ref, rows=512):
    for r0 in range(0, src_ref.shape[0], rows):
        dst_ref[r0:r0 + rows, :] = src_ref[r0:r0 + rows, :].astype(dst_ref.dtype)


def _ffn_up_kernel(h_ref, wg32_ref, wu32_ref, cwg_ref, cwu_ref, cbg_ref, cbu_ref, o_ref, pg_ref, pu_ref,
                   wg_ref, wu_ref, *, tiles_per_seq, chunk, rows):
    i = pl.program_id(1)
    tm, tn = o_ref.shape

    @pl.when(i == 0)
    def _():
        _cast_rows(wg32_ref, wg_ref)
        _cast_rows(wu32_ref, wu_ref)

    @pl.when(i % tiles_per_seq == 0)
    def _():
        pg_ref[...] = jnp.zeros(pg_ref.shape, F32)
        pu_ref[...] = jnp.zeros(pu_ref.shape, F32)

    for c0 in range(0, tn, chunk):
        sl = slice(c0, c0 + chunk)
        prev_g, prev_u = pg_ref[:, sl], pu_ref[:, sl]
        for r0 in range(0, tm, rows):
            rs = slice(r0, r0 + rows)
            ag = _dot(h_ref[rs, :], wg_ref[:, sl])
            au = _dot(h_ref[rs, :], wu_ref[:, sl])
            gate, gate_head = _causal_conv(ag, prev_g, cwg_ref[:, sl], cbg_ref[:, sl])
            up, up_head = _causal_conv(au, prev_u, cwu_ref[:, sl], cbu_ref[:, sl])
            o_ref[rs, sl] = (jax.nn.silu(gate) * up).astype(o_ref.dtype)
            o_ref[r0:r0 + SUBLANES, sl] = (jax.nn.silu(gate_head) * up_head).astype(o_ref.dtype)
            prev_g, prev_u = ag[rows - SUBLANES:rows], au[rows - SUBLANES:rows]
        pg_ref[:, sl] = prev_g
        pu_ref[:, sl] = prev_u


def ffn_up(h, w_up, layer, conv_w, conv_b, seq, tm=512, tn=512, chunk=MXU_COLS, rows=256):
    t, k = h.shape
    f = w_up.shape[2] // 2
    tm, tn = min(tm, seq), min(tn, f)
    chunk, rows = min(chunk, tn), min(rows, tm)
    nj = f // tn
    assert seq % tm == 0 and f % tn == 0 and tn % chunk == 0 and tm % rows == 0
    cb = conv_b.reshape(1, 2 * f)
    return pl.pallas_call(
        functools.partial(_ffn_up_kernel, tiles_per_seq=seq // tm, chunk=chunk, rows=rows),
        out_shape=jax.ShapeDtypeStruct((t, f), BF16),
        grid=(nj, t // tm),
        in_specs=[pl.BlockSpec((tm, k), lambda j, i: (i, 0)),
                  pl.BlockSpec((None, k, tn), lambda j, i: (layer, 0, j)),
                  pl.BlockSpec((None, k, tn), lambda j, i: (layer, 0, nj + j)),
                  pl.BlockSpec((CONV_WIDTH, tn), lambda j, i: (0, j)),
                  pl.BlockSpec((CONV_WIDTH, tn), lambda j, i: (0, nj + j)),
                  pl.BlockSpec((1, tn), lambda j, i: (0, j)),
                  pl.BlockSpec((1, tn), lambda j, i: (0, nj + j))],
        out_specs=pl.BlockSpec((tm, tn), lambda j, i: (i, j)),
        scratch_shapes=[pltpu.VMEM((SUBLANES, tn), F32), pltpu.VMEM((SUBLANES, tn), F32),
                        pltpu.VMEM((k, tn), BF16), pltpu.VMEM((k, tn), BF16)],
        compiler_params=_params(("parallel", "arbitrary")),
        name="ffn_up_conv_act",
    )(h, w_up, w_up, conv_w, conv_w, cb, cb)


def conv_ffn_block(x, norm_g, w_up, w_down, layer, conv_w, conv_b, seq):
    h = rmsnorm(x, norm_g, BF16)
    act = ffn_up(h, w_up, layer, conv_w, conv_b, seq)
    return matmul_residual(act, w_down, layer, x, tm=512, tn=256)


def hybrid_attention_block(x, positions, norm_g, w_in, w_out, layer, bsz, seq):
    h = rmsnorm(x, norm_g, BF16)
    cos, sin_lo, sin_hi = rotary_tables(positions)
    nh = N_HEADS_QKV
    qkv = qkv_projection(h, w_in, layer, cos, sin_lo, sin_hi,
                         n_q_cols=nh * HEAD_DIM, n_rot_cols=2 * nh * HEAD_DIM)
    o = moba_attention(qkv, bsz, seq, N_HEADS_A, 0, nh, 2 * nh, N_HEADS_A + N_HEADS_B_GROUP)
    o = dilated_attention(qkv, o, bsz, seq, N_HEADS_A, nh + N_HEADS_A, 2 * nh + N_HEADS_A, N_HEADS_A)
    return matmul_residual(o, w_out, layer, x, tm=1024, tn=1024)


def spatial_gating_block(x, norm_g, w_in, w_out, layer, b_in, v_gain, v_bias, w_s, b_s):
    h = rmsnorm(x, norm_g, BF16)
    z = matmul_bias_gelu(h, w_in, layer, b_in)
    gated = spatial_gating(z, v_gain, v_bias, w_s, b_s)
    return matmul_residual(gated, w_out, layer, x, tm=1024, tn=1024)


def kernel(x, positions, attn_norm, attn_w_in, attn_w_out, sg_norm, sg_w_in, sg_b_in, sg_v_gain,
           sg_v_bias, sg_w_s, sg_b_s, sg_w_out, ffn_norm, ffn_w_up, ffn_conv_w, ffn_conv_b,
           ffn_w_down, final_norm):
    bsz, seq, d = x.shape
    depth = ffn_norm.shape[0]
    attn_w_out, sg_w_in, sg_w_out, ffn_w_down = (
        w.astype(BF16) for w in (attn_w_out, sg_w_in, sg_w_out, ffn_w_down))
    h = x.reshape(bsz * seq, d)
    for layer in range(depth):
        i = layer // 2
        if layer % 2 == 0:
            h = hybrid_attention_block(h, positions, attn_norm[i], attn_w_in, attn_w_out, i, bsz, seq)
        else:
            h = spatial_gating_block(h, sg_norm[i], sg_w_in, sg_w_out, i, sg_b_in[i], sg_v_gain[i],
                                     sg_v_bias[i], sg_w_s[i], sg_b_s[i])
        h = conv_ffn_block(h, ffn_norm[layer], ffn_w_up, ffn_w_down, layer, ffn_conv_w[layer],
                           ffn_conv_b[layer], seq)
    return rmsnorm(h, final_norm, x.dtype).reshape(bsz, seq, d)
```

```python
import functools
import math

import jax
import jax.numpy as jnp
from jax import lax
from jax.experimental import pallas as pl
from jax.experimental.pallas import tpu as pltpu

F32 = jnp.float32
BF16 = jnp.bfloat16

HEAD_DIM = 128
ROT_DIM = HEAD_DIM // 4
ROPE_THETA = 500000.0
N_HEADS_A = 24
N_HEADS_B_GROUP = 8
B_PATTERNS = ((128, 1), (512, 4), (2048, 16))
N_HEADS_QKV = N_HEADS_A + N_HEADS_B_GROUP * len(B_PATTERNS)
MOBA_BLOCK = 256
MOBA_TOPK = 3
GMLP_CHUNK = 128
GMLP_GROUP_DIM = 128
CONV_WIDTH = 3
NORM_EPS = 1e-5

LANES = 128
SUBLANES = 8
MXU_COLS = 256
VMEM_LIMIT_BYTES = 56 * 1024 * 1024

MASK_VALUE = -1e30
LOG2_E = math.log2(math.e)
Q_SCALE = HEAD_DIM ** -0.5 * LOG2_E


def _params(semantics):
    return pltpu.CompilerParams(dimension_semantics=semantics, vmem_limit_bytes=VMEM_LIMIT_BYTES)


def _dot(a, b):
    return jnp.dot(a, b, preferred_element_type=F32)


def _dot_nt(a, b):
    return lax.dot_general(a, b, (((1,), (1,)), ((), ())), preferred_element_type=F32)


def _dot_tn(a, b):
    return lax.dot_general(a, b, (((0,), (0,)), ((), ())), preferred_element_type=F32)


def _head_slice(h):
    return slice(h * HEAD_DIM, (h + 1) * HEAD_DIM)


def _rmsnorm_kernel(x_ref, g_ref, o_ref):
    x = x_ref[...]
    y = x * lax.rsqrt(jnp.mean(x * x, axis=-1, keepdims=True) + NORM_EPS)
    o_ref[...] = (y * g_ref[...]).astype(o_ref.dtype)


def rmsnorm(x, g, out_dtype, rows=256):
    t, d = x.shape
    rows = min(rows, t)
    return pl.pallas_call(
        _rmsnorm_kernel,
        out_shape=jax.ShapeDtypeStruct((t, d), out_dtype),
        grid=(t // rows,),
        in_specs=[pl.BlockSpec((rows, d), lambda i: (i, 0)),
                  pl.BlockSpec((1, d), lambda i: (0, 0))],
        out_specs=pl.BlockSpec((rows, d), lambda i: (i, 0)),
        compiler_params=_params(("parallel",)),
        name="rmsnorm",
    )(x, g.reshape(1, d))


def _qkv_kernel(a_ref, w32_ref, cos_ref, sin_lo_ref, sin_hi_ref, o_ref, w_ref, *, n_q_tiles, n_rot_tiles):
    col, cur = _stream_weight_rows(w32_ref, w_ref)
    tn = o_ref.shape[1]

    @pl.when((col >= 0) & (col < n_rot_tiles))
    def _():
        sc = jnp.where(col < n_q_tiles, Q_SCALE, 1.0).astype(F32)
        cos = cos_ref[...] * sc
        sin_lo = sin_lo_ref[...] * sc
        sin_hi = sin_hi_ref[...] * sc
        for c0 in range(0, tn, MXU_COLS):
            x2 = _dot(a_ref[...], w_ref[cur, :, c0:c0 + MXU_COLS])
            for h in range(MXU_COLS // HEAD_DIM):
                x = x2[:, _head_slice(h)]
                up = pltpu.roll(x, HEAD_DIM - ROT_DIM // 2, axis=1)
                down = pltpu.roll(x, ROT_DIM // 2, axis=1)
                lo = c0 + h * HEAD_DIM
                o_ref[:, lo:lo + HEAD_DIM] = (x * cos + up * sin_lo + down * sin_hi).astype(o_ref.dtype)

    @pl.when(col >= n_rot_tiles)
    def _():
        o_ref[...] = _dot(a_ref[...], w_ref[cur]).astype(o_ref.dtype)


def qkv_projection(h, w, layer, cos, sin_lo, sin_hi, n_q_cols, n_rot_cols, tm=1024, tn=1024):
    t, k = h.shape
    n = w.shape[2]
    tm, tn = min(tm, t), min(tn, n)
    ni, nj = t // tm, n // tn
    assert n_q_cols % tn == 0 and n_rot_cols % tn == 0 and tn % MXU_COLS == 0
    return pl.pallas_call(
        functools.partial(_qkv_kernel, n_q_tiles=n_q_cols // tn, n_rot_tiles=n_rot_cols // tn),
        out_shape=jax.ShapeDtypeStruct((t, n), BF16),
        grid=(nj + 1, ni),
        in_specs=[pl.BlockSpec((tm, k), lambda j, i: (i, 0)),
                  _streamed_weight_spec(layer, k, tn, ni, nj, 0),
                  pl.BlockSpec((tm, HEAD_DIM), lambda j, i: (i, 0)),
                  pl.BlockSpec((tm, HEAD_DIM), lambda j, i: (i, 0)),
                  pl.BlockSpec((tm, HEAD_DIM), lambda j, i: (i, 0))],
        out_specs=_streamed_out_spec(tm, tn),
        scratch_shapes=[pltpu.VMEM((2, k, tn), BF16)],
        compiler_params=_params(("arbitrary", "arbitrary")),
        name="qkv_rotary",
    )(h, w, cos, sin_lo, sin_hi)


def rotary_tables(positions):
    half = ROT_DIM // 2
    inv_freq = jnp.power(ROPE_THETA, -jnp.arange(half, dtype=F32) * (2.0 / ROT_DIM))
    ang = positions.reshape(-1).astype(F32)[:, None] * inv_freq
    cos, sin = jnp.cos(ang), jnp.sin(ang)
    t = ang.shape[0]
    ones = jnp.ones((t, HEAD_DIM - ROT_DIM), F32)
    zeros = jnp.zeros((t, HEAD_DIM - ROT_DIM), F32)
    zh = jnp.zeros((t, half), F32)
    cos_t = jnp.concatenate([cos, cos, ones], axis=1)
    sin_lo = jnp.concatenate([-sin, zh, zeros], axis=1)
    sin_hi = jnp.concatenate([zh, sin, zeros], axis=1)
    return cos_t, sin_lo, sin_hi


def _mm_resid_kernel(a_ref, w_ref, r_ref, o_ref):
    o_ref[...] = r_ref[...] + _dot(a_ref[...], w_ref[...])


def matmul_residual(a, w, layer, resid, tm, tn):
    t, k = a.shape
    n = w.shape[2]
    tm, tn = min(tm, t), min(tn, n)
    return pl.pallas_call(
        _mm_resid_kernel,
        out_shape=jax.ShapeDtypeStruct((t, n), F32),
        grid=(t // tm, n // tn),
        in_specs=[pl.BlockSpec((tm, k), lambda i, j: (i, 0)),
                  pl.BlockSpec((None, k, tn), lambda i, j: (layer, 0, j)),
                  pl.BlockSpec((tm, tn), lambda i, j: (i, j))],
        out_specs=pl.BlockSpec((tm, tn), lambda i, j: (i, j)),
        compiler_params=_params(("parallel", "arbitrary")),
        name="matmul_residual",
    )(a, w, resid)


def _mm_bias_gelu_kernel(a_ref, w_ref, b_ref, o_ref):
    z = _dot(a_ref[...], w_ref[...]) + b_ref[...]
    o_ref[...] = (0.5 * z * (1.0 + lax.erf(z * (2.0 ** -0.5)))).astype(o_ref.dtype)


def matmul_bias_gelu(a, w, layer, b, tm=1024, tn=1024):
    t, k = a.shape
    n = w.shape[2]
    tm, tn = min(tm, t), min(tn, n)
    return pl.pallas_call(
        _mm_bias_gelu_kernel,
        out_shape=jax.ShapeDtypeStruct((t, n), BF16),
        grid=(t // tm, n // tn),
        in_specs=[pl.BlockSpec((tm, k), lambda i, j: (i, 0)),
                  pl.BlockSpec((None, k, tn), lambda i, j: (layer, 0, j)),
                  pl.BlockSpec((1, tn), lambda i, j: (0, j))],
        out_specs=pl.BlockSpec((tm, tn), lambda i, j: (i, j)),
        compiler_params=_params(("parallel", "arbitrary")),
        name="matmul_bias_gelu",
    )(a, w, b.reshape(1, n))


def _ones_augmented(v):
    return jnp.concatenate([v, jnp.ones_like(v)], axis=1)


def _softmax_init(s, vaug):
    m = jnp.max(s, axis=1, keepdims=True)
    p = jnp.exp2(s - m)
    return m, _dot(p.astype(vaug.dtype), vaug)


def _softmax_update(state, s, vaug):
    m, accl = state
    m_new = jnp.maximum(m, jnp.max(s, axis=1, keepdims=True))
    alpha = jnp.exp2(m - m_new)
    p = jnp.exp2(s - m_new)
    return m_new, alpha * accl + _dot(p.astype(vaug.dtype), vaug)


def _softmax_finish(state):
    _, accl = state
    return accl[:, :HEAD_DIM] / accl[:, HEAD_DIM:]


MOBA_HEADS_PER_STEP = 4
MOBA_BLOCKS_PER_ITER = 2


def _moba_kernel(q_ref, k_ref, v_ref, o_ref, kmean_ref, vaug_ref, *, n_blocks, heads):
    i = pl.program_id(2)
    blk = MOBA_BLOCK
    per_iter = MOBA_BLOCKS_PER_ITER

    @pl.when(i == 0)
    def _():
        for h in range(heads):
            hs = _head_slice(h)
            kf = k_ref[:, hs].astype(F32).reshape(n_blocks, blk, HEAD_DIM)
            kmean_ref[h] = jnp.mean(kf, axis=1)
            vaug_ref[h] = _ones_augmented(v_ref[:, hs])

    row = lax.broadcasted_iota(jnp.int32, (blk, blk), 0)
    col = lax.broadcasted_iota(jnp.int32, (blk, blk), 1)
    causal = col <= row
    blk_idx = lax.broadcasted_iota(jnp.int32, (n_blocks, blk), 0)
    pow2 = jnp.left_shift(1, lax.broadcasted_iota(jnp.int32, (n_blocks, LANES), 0)).astype(BF16)
    own = pl.ds(pl.multiple_of(i * blk, blk), blk)

    def select_bits(h):
        q = q_ref[:, _head_slice(h)]
        km = kmean_ref[h]
        km_hi = km.astype(BF16)
        km_lo = (km - km_hi.astype(F32)).astype(BF16)
        gate = _dot_nt(km_hi, q) + _dot_nt(km_lo, q)
        rank = jnp.zeros(gate.shape, jnp.int32)
        for m in range(n_blocks):
            gm = gate[m:m + 1, :]
            beats = (gm > gate) | ((gm == gate) & (m < blk_idx))
            rank = rank + jnp.where(beats & (m < i), 1, 0)
        selected = (blk_idx < i) & (rank < MOBA_TOPK)
        return _dot_tn(selected.astype(BF16), pow2).astype(jnp.int32)

    bits = [select_bits(h) for h in range(heads)]

    states = []
    for h in range(heads):
        hs = _head_slice(h)
        s = jnp.where(causal, _dot_nt(q_ref[:, hs], k_ref[own, hs]), MASK_VALUE)
        states.append(_softmax_init(s, vaug_ref[h, own, :]))

    def body(c, states):
        rows = pl.ds(pl.multiple_of(c * (per_iter * blk), per_iter * blk), per_iter * blk)
        new_states = []
        for h in range(heads):
            hs = _head_slice(h)
            s = _dot_nt(q_ref[:, hs], k_ref[rows, hs])
            bias = []
            for t in range(per_iter):
                keep = (jnp.right_shift(bits[h], c * per_iter + t) & 1) == 1
                bias += [jnp.where(keep, 0.0, MASK_VALUE)] * (blk // LANES)
            s = s + jnp.concatenate(bias, axis=1)
            new_states.append(_softmax_update(states[h], s, vaug_ref[h, rows, :]))
        return tuple(new_states)

    n_iters = (i + per_iter - 1) // per_iter
    states = lax.fori_loop(0, n_iters, body, tuple(states))
    for h in range(heads):
        o_ref[:, _head_slice(h)] = _softmax_finish(states[h]).astype(o_ref.dtype)


def moba_attention(qkv, bsz, seq, n_heads, q_head0, k_head0, v_head0, n_out_heads):
    nb = seq // MOBA_BLOCK
    hp = MOBA_HEADS_PER_STEP
    assert n_heads % hp == 0 and q_head0 % hp == 0 and k_head0 % hp == 0 and v_head0 % hp == 0
    assert nb % MOBA_BLOCKS_PER_ITER == 0
    w = hp * HEAD_DIM
    return pl.pallas_call(
        functools.partial(_moba_kernel, n_blocks=nb, heads=hp),
        out_shape=jax.ShapeDtypeStruct((bsz * seq, n_out_heads * HEAD_DIM), BF16),
        grid=(bsz, n_heads // hp, nb),
        in_specs=[pl.BlockSpec((MOBA_BLOCK, w), lambda b, h, i: (b * nb + i, q_head0 // hp + h)),
                  pl.BlockSpec((seq, w), lambda b, h, i: (b, k_head0 // hp + h)),
                  pl.BlockSpec((seq, w), lambda b, h, i: (b, v_head0 // hp + h))],
        out_specs=pl.BlockSpec((MOBA_BLOCK, w), lambda b, h, i: (b * nb + i, h)),
        scratch_shapes=[pltpu.VMEM((hp, nb, HEAD_DIM), F32),
                        pltpu.VMEM((hp, seq, 2 * HEAD_DIM), BF16)],
        compiler_params=_params(("parallel", "parallel", "arbitrary")),
        name="moba_attention",
    )(qkv, qkv, qkv)


DIL_TILE = 256
DIL_BACK = 128


def _residue_permutation(dil):
    per = DIL_TILE // dil
    r = lax.broadcasted_iota(jnp.int32, (DIL_TILE, DIL_TILE), 0)
    c = lax.broadcasted_iota(jnp.int32, (DIL_TILE, DIL_TILE), 1)
    return (c == (r % per) * dil + r // per).astype(BF16)


def _band_tile(q, k, vaug, q0, k0):
    n = k.shape[0]
    s = _dot_nt(q, k)
    dist = (q0 - k0) + (lax.broadcasted_iota(jnp.int32, (DIL_TILE, n), 0)
                        - lax.broadcasted_iota(jnp.int32, (DIL_TILE, n), 1))
    s = jnp.where((dist >= 0) & (dist <= DIL_BACK), s, MASK_VALUE)
    m = jnp.max(s, axis=1, keepdims=True)
    p = jnp.exp2(s - m)
    ol = _dot(p.astype(vaug.dtype), vaug)
    l = ol[:, HEAD_DIM:]
    return ol[:, :HEAD_DIM] / l, m + jnp.log(l) * LOG2_E


def _dilated_kernel(q0_ref, q1_ref, q2_ref, k0_ref, k1_ref, k2_ref, v0_ref, v1_ref, v2_ref, _buf_ref,
                    o_ref, v0aug_ref, qs1_ref, ks1_ref, vs1_ref, qs2_ref, ks2_ref, vs2_ref, out_ref, lse_ref, *, seq):
    tile = DIL_TILE
    n_tiles = seq // tile
    keys = tile + DIL_BACK
    groups = ((q1_ref, k1_ref, v1_ref, qs1_ref, ks1_ref, vs1_ref),
              (q2_ref, k2_ref, v2_ref, qs2_ref, ks2_ref, vs2_ref))

    v0aug_ref[...] = _ones_augmented(v0_ref[...])
    for (_, dil), refs in zip(B_PATTERNS[1:], groups):
        per = tile // dil
        perm = _residue_permutation(dil)

        def gather(b, carry, per=per, perm=perm, refs=refs, dil=dil):
            q_ref, k_ref, v_ref, qs_ref, ks_ref, vs_ref = refs
            rows = pl.ds(pl.multiple_of(b * tile, tile), tile)
            dst = pl.ds(pl.multiple_of(b * per, per), per)
            for src, out in ((q_ref, qs_ref), (k_ref, ks_ref)):
                out[:, dst, :] = _dot(perm, src[rows, :]).astype(BF16).reshape(dil, per, HEAD_DIM)
            vp = _dot(perm, v_ref[rows, :]).astype(BF16)
            vs_ref[:, dst, :] = _ones_augmented(vp).reshape(dil, per, 2 * HEAD_DIM)
            return carry

        lax.fori_loop(0, n_tiles, gather, 0)

    def window(t0):
        k0 = jnp.maximum(t0 - DIL_BACK, 0)
        return k0, pl.ds(pl.multiple_of(k0, DIL_BACK), keys)

    def tile_g0(i, carry):
        t0 = pl.multiple_of(i * tile, tile)
        k0, krows = window(t0)
        out, lse = _band_tile(q0_ref[pl.ds(t0, tile), :], k0_ref[krows, :], v0aug_ref[krows, :], t0, k0)
        out_ref[0, pl.ds(t0, tile), :] = out
        lse_ref[0, pl.ds(t0, tile), :] = lse
        return carry

    lax.fori_loop(0, n_tiles, tile_g0, 0, unroll=2)

    for g, ((_, dil), refs) in enumerate(zip(B_PATTERNS[1:], groups), start=1):
        sub_tiles = n_tiles // dil

        def tile_g(it, carry, g=g, dil=dil, sub_tiles=sub_tiles, refs=refs[3:]):
            qs_ref, ks_ref, vs_ref = refs
            res, ti = it // sub_tiles, it % sub_tiles
            t0 = pl.multiple_of(ti * tile, tile)
            if sub_tiles == 1:
                k0, krows = 0, pl.ds(0, tile)
            else:
                k0, krows = window(t0)
            out, lse = _band_tile(qs_ref[res, pl.ds(t0, tile), :], ks_ref[res, krows, :],
                                  vs_ref[res, krows, :], t0, k0)
            natural = pl.ds(t0 * dil + res, tile, stride=dil)
            out_ref[g, natural, :] = out
            lse_ref[g, natural, :] = lse
            return carry

        lax.fori_loop(0, n_tiles, tile_g, 0, unroll=2)

    def merge(i, carry):
        rows = pl.ds(pl.multiple_of(i * tile, tile), tile)
        lses = [lse_ref[g, rows, :] for g in range(len(B_PATTERNS))]
        top = functools.reduce(jnp.maximum, lses)
        ws = [jnp.exp2(lse - top) for lse in lses]
        num = sum(w * out_ref[g, rows, :] for g, w in enumerate(ws))
        o_ref[rows, :] = (num / sum(ws)).astype(o_ref.dtype)
        return carry

    lax.fori_loop(0, n_tiles, merge, 0)


def dilated_attention(qkv, out_buf, bsz, seq, q_head0, k_head0, v_head0, out_head0):
    hg = N_HEADS_B_GROUP
    ng = len(B_PATTERNS)
    tile = DIL_TILE
    assert all(w // d == DIL_BACK and tile % d == 0 and (seq // tile) % d == 0 for w, d in B_PATTERNS)
    assert B_PATTERNS[0][1] == 1 and seq % tile == 0 and seq >= tile + DIL_BACK

    def spec(head0, g):
        return pl.BlockSpec((seq, HEAD_DIM), lambda b, h, g=g: (b, head0 + g * hg + h))

    scratch = [pltpu.VMEM((seq, 2 * HEAD_DIM), BF16)]
    for _, dil in B_PATTERNS[1:]:
        scratch += [pltpu.VMEM((dil, seq // dil, HEAD_DIM), BF16), pltpu.VMEM((dil, seq // dil, HEAD_DIM), BF16),
                    pltpu.VMEM((dil, seq // dil, 2 * HEAD_DIM), BF16)]
    scratch += [pltpu.VMEM((ng, seq, HEAD_DIM), F32), pltpu.VMEM((ng, seq, HEAD_DIM), F32)]
    return pl.pallas_call(
        functools.partial(_dilated_kernel, seq=seq),
        out_shape=jax.ShapeDtypeStruct(out_buf.shape, out_buf.dtype),
        grid=(bsz, hg),
        in_specs=([spec(q_head0, g) for g in range(ng)] + [spec(k_head0, g) for g in range(ng)]
                  + [spec(v_head0, g) for g in range(ng)] + [pl.BlockSpec(memory_space=pl.ANY)]),
        out_specs=pl.BlockSpec((seq, HEAD_DIM), lambda b, h: (b, out_head0 + h)),
        scratch_shapes=scratch,
        input_output_aliases={3 * ng: 0},
        compiler_params=_params(("parallel", "parallel")),
        name="dilated_attention",
    )(*([qkv] * (3 * ng)), out_buf)


def _sgu_kernel(u_ref, v_ref, gain_ref, bias_ref, ws_ref, bs_ref, o_ref):
    v = v_ref[...].astype(F32)
    mu = jnp.mean(v, axis=-1, keepdims=True)
    vc = v - mu
    vn = vc * lax.rsqrt(jnp.mean(vc * vc, axis=-1, keepdims=True) + NORM_EPS)
    vn = (vn * gain_ref[...] + bias_ref[...]).astype(BF16)
    c = GMLP_CHUNK
    row = lax.broadcasted_iota(jnp.int32, (c, c), 0)
    col = lax.broadcasted_iota(jnp.int32, (c, c), 1)
    causal = col <= row
    for g in range(ws_ref.shape[0]):
        sl = slice(g * GMLP_GROUP_DIM, (g + 1) * GMLP_GROUP_DIM)
        w = jnp.where(causal, ws_ref[g], 0.0).astype(BF16)
        f = _dot(w, vn[:, sl]) + bs_ref[:, g:g + 1]
        o_ref[:, sl] = (u_ref[:, sl].astype(F32) * f).astype(o_ref.dtype)


def spatial_gating(z, gain, bias, w_s, b_s):
    t, e2 = z.shape
    e = e2 // 2
    groups = w_s.shape[0]
    c = GMLP_CHUNK
    return pl.pallas_call(
        _sgu_kernel,
        out_shape=jax.ShapeDtypeStruct((t, e), BF16),
        grid=(t // c,),
        in_specs=[pl.BlockSpec((c, e), lambda i: (i, 0)),
                  pl.BlockSpec((c, e), lambda i: (i, 1)),
                  pl.BlockSpec((1, e), lambda i: (0, 0)),
                  pl.BlockSpec((1, e), lambda i: (0, 0)),
                  pl.BlockSpec((groups, c, c), lambda i: (0, 0, 0)),
                  pl.BlockSpec((c, groups), lambda i: (0, 0))],
        out_specs=pl.BlockSpec((c, e), lambda i: (i, 0)),
        compiler_params=_params(("parallel",)),
        name="spatial_gating",
    )(z, z, gain.reshape(1, e), bias.reshape(1, e), w_s, b_s.T)


def _causal_conv(a, prev, cw, cb):
    body = cb + a * cw[CONV_WIDTH - 1:CONV_WIDTH]
    first = a[0:SUBLANES]
    ext = jnp.concatenate([prev, first], axis=0)
    head = cb + first * cw[CONV_WIDTH - 1:CONV_WIDTH]
    for back in range(1, CONV_WIDTH):
        tap = cw[CONV_WIDTH - 1 - back:CONV_WIDTH - back]
        body = body + pltpu.roll(a, back, axis=0) * tap
        head = head + pltpu.roll(ext, back, axis=0)[SUBLANES:] * tap
    return body, head


def _streamed_weight_spec(layer, k, tn, ni, nj, col0):
    assert k % ni == 0
    return pl.BlockSpec((None, k // ni, tn), lambda j, i: (layer, i, col0 + jnp.minimum(j, nj - 1)))


def _streamed_out_spec(tm, tn):
    return pl.BlockSpec((tm, tn), lambda j, i: (jnp.where(j == 0, 0, i), jnp.maximum(j - 1, 0)))


def _stream_weight_rows(*refs):
    j, i = pl.program_id(0), pl.program_id(1)
    fill = j % 2
    for w32_ref, w_ref in zip(refs[0::2], refs[1::2]):
        chunk = w32_ref.shape[0]
        w_ref[fill, pl.ds(pl.multiple_of(i * chunk, chunk), chunk), :] = w32_ref[...].astype(w_ref.dtype)
    return j - 1, 1 - fill


def _ffn_up_kernel(h_ref, wg32_ref, wu32_ref, cwg_ref, cwu_ref, cbg_ref, cbu_ref, o_ref, pg_ref, pu_ref,
                   wg_ref, wu_ref, *, tiles_per_seq, chunk, rows):
    i = pl.program_id(1)
    tm, tn = o_ref.shape
    col, cur = _stream_weight_rows(wg32_ref, wg_ref, wu32_ref, wu_ref)

    @pl.when(i % tiles_per_seq == 0)
    def _():
        pg_ref[...] = jnp.zeros(pg_ref.shape, F32)
        pu_ref[...] = jnp.zeros(pu_ref.shape, F32)

    @pl.when(col >= 0)
    def _():
        for c0 in range(0, tn, chunk):
            sl = slice(c0, c0 + chunk)
            prev_g, prev_u = pg_ref[:, sl], pu_ref[:, sl]
            for r0 in range(0, tm, rows):
                rs = slice(r0, r0 + rows)
                ag = _dot(h_ref[rs, :], wg_ref[cur, :, sl])
                au = _dot(h_ref[rs, :], wu_ref[cur, :, sl])
                gate, gate_head = _causal_conv(ag, prev_g, cwg_ref[:, sl], cbg_ref[:, sl])
                up, up_head = _causal_conv(au, prev_u, cwu_ref[:, sl], cbu_ref[:, sl])
                o_ref[rs, sl] = (jax.nn.silu(gate) * up).astype(o_ref.dtype)
                o_ref[r0:r0 + SUBLANES, sl] = (jax.nn.silu(gate_head) * up_head).astype(o_ref.dtype)
                prev_g, prev_u = ag[rows - SUBLANES:rows], au[rows - SUBLANES:rows]
            pg_ref[:, sl] = prev_g
            pu_ref[:, sl] = prev_u


def ffn_up(h, w_up, layer, conv_w, conv_b, seq, tm=1024, tn=512, chunk=MXU_COLS, rows=512):
    t, k = h.shape
    f = w_up.shape[2] // 2
    tm, tn = min(tm, seq), min(tn, f)
    chunk, rows = min(chunk, tn), min(rows, tm)
    ni, nj = t // tm, f // tn
    assert seq % tm == 0 and f % tn == 0 and tn % chunk == 0 and tm % rows == 0
    cb = conv_b.reshape(1, 2 * f)

    def col_spec(rows_, col0):
        return pl.BlockSpec((rows_, tn), lambda j, i: (0, col0 + jnp.maximum(j - 1, 0)))

    return pl.pallas_call(
        functools.partial(_ffn_up_kernel, tiles_per_seq=seq // tm, chunk=chunk, rows=rows),
        out_shape=jax.ShapeDtypeStruct((t, f), BF16),
        grid=(nj + 1, ni),
        in_specs=[pl.BlockSpec((tm, k), lambda j, i: (i, 0)),
                  _streamed_weight_spec(layer, k, tn, ni, nj, 0),
                  _streamed_weight_spec(layer, k, tn, ni, nj, nj),
                  col_spec(CONV_WIDTH, 0), col_spec(CONV_WIDTH, nj),
                  col_spec(1, 0), col_spec(1, nj)],
        out_specs=_streamed_out_spec(tm, tn),
        scratch_shapes=[pltpu.VMEM((SUBLANES, tn), F32), pltpu.VMEM((SUBLANES, tn), F32),
                        pltpu.VMEM((2, k, tn), BF16), pltpu.VMEM((2, k, tn), BF16)],
        compiler_params=_params(("arbitrary", "arbitrary")),
        name="ffn_up_conv_act",
    )(h, w_up, w_up, conv_w, conv_w, cb, cb)


def conv_ffn_block(x, norm_g, w_up, w_down, layer, conv_w, conv_b, seq):
    h = rmsnorm(x, norm_g, BF16)
    act = ffn_up(h, w_up, layer, conv_w, conv_b, seq)
    return matmul_residual(act, w_down, layer, x, tm=512, tn=256)


def hybrid_attention_block(x, positions, norm_g, w_in, w_out, layer, bsz, seq):
    h = rmsnorm(x, norm_g, BF16)
    cos, sin_lo, sin_hi = rotary_tables(positions)
    nh = N_HEADS_QKV
    qkv = qkv_projection(h, w_in, layer, cos, sin_lo, sin_hi,
                         n_q_cols=nh * HEAD_DIM, n_rot_cols=2 * nh * HEAD_DIM)
    o = moba_attention(qkv, bsz, seq, N_HEADS_A, 0, nh, 2 * nh, N_HEADS_A + N_HEADS_B_GROUP)
    o = dilated_attention(qkv, o, bsz, seq, N_HEADS_A, nh + N_HEADS_A, 2 * nh + N_HEADS_A, N_HEADS_A)
    return matmul_residual(o, w_out, layer, x, tm=1024, tn=1024)


def spatial_gating_block(x, norm_g, w_in, w_out, layer, b_in, v_gain, v_bias, w_s, b_s):
    h = rmsnorm(x, norm_g, BF16)
    z = matmul_bias_gelu(h, w_in, layer, b_in)
    gated = spatial_gating(z, v_gain, v_bias, w_s, b_s)
    return matmul_residual(gated, w_out, layer, x, tm=1024, tn=1024)


def kernel(x, positions, attn_norm, attn_w_in, attn_w_out, sg_norm, sg_w_in, sg_b_in, sg_v_gain,
           sg_v_bias, sg_w_s, sg_b_s, sg_w_out, ffn_norm, ffn_w_up, ffn_conv_w, ffn_conv_b,
           ffn_w_down, final_norm):
    bsz, seq, d = x.shape
    depth = ffn_norm.shape[0]
    attn_w_out, sg_w_in, sg_w_out, ffn_w_down = (
        w.astype(BF16) for w in (attn_w_out, sg_w_in, sg_w_out, ffn_w_down))
    h = x.reshape(bsz * seq, d)
    for layer in range(depth):
        i = layer // 2
        if layer % 2 == 0:
            h = hybrid_attention_block(h, positions, attn_norm[i], attn_w_in, attn_w_out, i, bsz, seq)
        else:
            h = spatial_gating_block(h, sg_norm[i], sg_w_in, sg_w_out, i, sg_b_in[i], sg_v_gain[i],
                                     sg_v_bias[i], sg_w_s[i], sg_b_s[i])
        h = conv_ffn_block(h, ffn_norm[layer], ffn_w_up, ffn_w_down, layer, ffn_conv_w[layer],
                           ffn_conv_b[layer], seq)
    return rmsnorm(h, final_norm, x.dtype).reshape(bsz, seq, d)
```

```python
import functools
import math

import jax
import jax.numpy as jnp
from jax import lax
from jax.experimental import pallas as pl
from jax.experimental.pallas import tpu as pltpu

F32 = jnp.float32
BF16 = jnp.bfloat16

HEAD_DIM = 128
ROT_DIM = HEAD_DIM // 4
ROPE_THETA = 500000.0
N_HEADS_A = 24
N_HEADS_B_GROUP = 8
B_PATTERNS = ((128, 1), (512, 4), (2048, 16))
N_HEADS_QKV = N_HEADS_A + N_HEADS_B_GROUP * len(B_PATTERNS)
MOBA_BLOCK = 256
MOBA_TOPK = 3
GMLP_CHUNK = 128
GMLP_GROUP_DIM = 128
CONV_WIDTH = 3
NORM_EPS = 1e-5

LANES = 128
SUBLANES = 8
MXU_COLS = 256
VMEM_LIMIT_BYTES = 56 * 1024 * 1024

MASK_VALUE = -1e30
LOG2_E = math.log2(math.e)
Q_SCALE = HEAD_DIM ** -0.5 * LOG2_E


def _params(semantics):
    return pltpu.CompilerParams(dimension_semantics=semantics, vmem_limit_bytes=VMEM_LIMIT_BYTES)


def _dot(a, b):
    return jnp.dot(a, b, preferred_element_type=F32)


def _dot_nt(a, b):
    return lax.dot_general(a, b, (((1,), (1,)), ((), ())), preferred_element_type=F32)


def _dot_tn(a, b):
    return lax.dot_general(a, b, (((0,), (0,)), ((), ())), preferred_element_type=F32)


def _head_slice(h):
    return slice(h * HEAD_DIM, (h + 1) * HEAD_DIM)


def _rmsnorm_kernel(x_ref, g_ref, o_ref):
    x = x_ref[...]
    y = x * lax.rsqrt(jnp.mean(x * x, axis=-1, keepdims=True) + NORM_EPS)
    o_ref[...] = (y * g_ref[...]).astype(o_ref.dtype)


def rmsnorm(x, g, out_dtype, rows=256):
    t, d = x.shape
    rows = min(rows, t)
    return pl.pallas_call(
        _rmsnorm_kernel,
        out_shape=jax.ShapeDtypeStruct((t, d), out_dtype),
        grid=(t // rows,),
        in_specs=[pl.BlockSpec((rows, d), lambda i: (i, 0)),
                  pl.BlockSpec((1, d), lambda i: (0, 0))],
        out_specs=pl.BlockSpec((rows, d), lambda i: (i, 0)),
        compiler_params=_params(("parallel",)),
        name="rmsnorm",
    )(x, g.reshape(1, d))


def _qkv_kernel(a_ref, w32_ref, cos_ref, sin_lo_ref, sin_hi_ref, o_ref, w_ref, *, n_q_tiles, n_rot_tiles):
    col, cur = _stream_weight_rows(w32_ref, w_ref)
    tn = o_ref.shape[1]

    @pl.when((col >= 0) & (col < n_rot_tiles))
    def _():
        sc = jnp.where(col < n_q_tiles, Q_SCALE, 1.0).astype(F32)
        cos = cos_ref[...] * sc
        sin_lo = sin_lo_ref[...] * sc
        sin_hi = sin_hi_ref[...] * sc
        for c0 in range(0, tn, MXU_COLS):
            x2 = _dot(a_ref[...], w_ref[cur, :, c0:c0 + MXU_COLS])
            for h in range(MXU_COLS // HEAD_DIM):
                x = x2[:, _head_slice(h)]
                up = pltpu.roll(x, HEAD_DIM - ROT_DIM // 2, axis=1)
                down = pltpu.roll(x, ROT_DIM // 2, axis=1)
                lo = c0 + h * HEAD_DIM
                o_ref[:, lo:lo + HEAD_DIM] = (x * cos + up * sin_lo + down * sin_hi).astype(o_ref.dtype)

    @pl.when(col >= n_rot_tiles)
    def _():
        o_ref[...] = _dot(a_ref[...], w_ref[cur]).astype(o_ref.dtype)


def qkv_projection(h, w, layer, cos, sin_lo, sin_hi, n_q_cols, n_rot_cols, tm=1024, tn=1024):
    t, k = h.shape
    n = w.shape[2]
    tm, tn = min(tm, t), min(tn, n)
    ni, nj = t // tm, n // tn
    assert n_q_cols % tn == 0 and n_rot_cols % tn == 0 and tn % MXU_COLS == 0
    return pl.pallas_call(
        functools.partial(_qkv_kernel, n_q_tiles=n_q_cols // tn, n_rot_tiles=n_rot_cols // tn),
        out_shape=jax.ShapeDtypeStruct((t, n), BF16),
        grid=(nj + 1, ni),
        in_specs=[pl.BlockSpec((tm, k), lambda j, i: (i, 0)),
                  _streamed_weight_spec(layer, k, tn, ni, nj, 0),
                  pl.BlockSpec((tm, HEAD_DIM), lambda j, i: (i, 0)),
                  pl.BlockSpec((tm, HEAD_DIM), lambda j, i: (i, 0)),
                  pl.BlockSpec((tm, HEAD_DIM), lambda j, i: (i, 0))],
        out_specs=_streamed_out_spec(tm, tn),
        scratch_shapes=[pltpu.VMEM((2, k, tn), BF16)],
        compiler_params=_params(("arbitrary", "arbitrary")),
        name="qkv_rotary",
    )(h, w, cos, sin_lo, sin_hi)


def rotary_tables(positions):
    half = ROT_DIM // 2
    inv_freq = jnp.power(ROPE_THETA, -jnp.arange(half, dtype=F32) * (2.0 / ROT_DIM))
    ang = positions.reshape(-1).astype(F32)[:, None] * inv_freq
    cos, sin = jnp.cos(ang), jnp.sin(ang)
    t = ang.shape[0]
    ones = jnp.ones((t, HEAD_DIM - ROT_DIM), F32)
    zeros = jnp.zeros((t, HEAD_DIM - ROT_DIM), F32)
    zh = jnp.zeros((t, half), F32)
    cos_t = jnp.concatenate([cos, cos, ones], axis=1)
    sin_lo = jnp.concatenate([-sin, zh, zeros], axis=1)
    sin_hi = jnp.concatenate([zh, sin, zeros], axis=1)
    return cos_t, sin_lo, sin_hi


def _mm_resid_kernel(a_ref, w_ref, r_ref, o_ref):
    o_ref[...] = r_ref[...] + _dot(a_ref[...], w_ref[...])


def matmul_residual(a, w, layer, resid, tm, tn, a_buffers=2):
    t, k = a.shape
    n = w.shape[2]
    tm, tn = min(tm, t), min(tn, n)
    return pl.pallas_call(
        _mm_resid_kernel,
        out_shape=jax.ShapeDtypeStruct((t, n), F32),
        grid=(t // tm, n // tn),
        in_specs=[pl.BlockSpec((tm, k), lambda i, j: (i, 0), pipeline_mode=pl.Buffered(a_buffers)),
                  pl.BlockSpec((None, k, tn), lambda i, j: (layer, 0, j)),
                  pl.BlockSpec((tm, tn), lambda i, j: (i, j))],
        out_specs=pl.BlockSpec((tm, tn), lambda i, j: (i, j)),
        compiler_params=_params(("parallel", "arbitrary")),
        name="matmul_residual",
    )(a, w, resid)


def _mm_bias_gelu_kernel(a_ref, w32_ref, b_ref, o_ref, w_ref):
    col, cur = _stream_weight_rows(w32_ref, w_ref)

    @pl.when(col >= 0)
    def _():
        z = _dot(a_ref[...], w_ref[cur]) + b_ref[...]
        o_ref[...] = (0.5 * z * (1.0 + lax.erf(z * (2.0 ** -0.5)))).astype(o_ref.dtype)


def matmul_bias_gelu(a, w, layer, b, tm=1024, tn=1024):
    t, k = a.shape
    n = w.shape[2]
    tm, tn = min(tm, t), min(tn, n)
    ni, nj = t // tm, n // tn
    return pl.pallas_call(
        _mm_bias_gelu_kernel,
        out_shape=jax.ShapeDtypeStruct((t, n), BF16),
        grid=(nj + 1, ni),
        in_specs=[pl.BlockSpec((tm, k), lambda j, i: (i, 0)),
                  _streamed_weight_spec(layer, k, tn, ni, nj, 0),
                  pl.BlockSpec((1, tn), lambda j, i: (0, jnp.maximum(j - 1, 0)))],
        out_specs=_streamed_out_spec(tm, tn),
        scratch_shapes=[pltpu.VMEM((2, k, tn), BF16)],
        compiler_params=_params(("arbitrary", "arbitrary")),
        name="matmul_bias_gelu",
    )(a, w, b.reshape(1, n))


def _ones_augmented(v):
    return jnp.concatenate([v, jnp.ones_like(v)], axis=1)


def _softmax_init(s, vaug):
    m = jnp.max(s, axis=1, keepdims=True)
    p = jnp.exp2(s - m)
    return m, _dot(p.astype(vaug.dtype), vaug)


def _softmax_update(state, s, vaug):
    m, accl = state
    m_new = jnp.maximum(m, jnp.max(s, axis=1, keepdims=True))
    alpha = jnp.exp2(m - m_new)
    p = jnp.exp2(s - m_new)
    return m_new, alpha * accl + _dot(p.astype(vaug.dtype), vaug)


def _softmax_finish(state):
    _, accl = state
    return accl[:, :HEAD_DIM] / accl[:, HEAD_DIM:]


MOBA_HEADS_PER_STEP = 4
MOBA_BLOCKS_PER_ITER = 2


def _moba_kernel(q_ref, k_ref, v_ref, o_ref, kmean_ref, vaug_ref, *, n_blocks, heads):
    i = pl.program_id(2)
    blk = MOBA_BLOCK
    per_iter = MOBA_BLOCKS_PER_ITER

    @pl.when(i == 0)
    def _():
        for h in range(heads):
            hs = _head_slice(h)
            kf = k_ref[:, hs].astype(F32).reshape(n_blocks, blk, HEAD_DIM)
            kmean_ref[h] = jnp.mean(kf, axis=1)
            vaug_ref[h] = _ones_augmented(v_ref[:, hs])

    row = lax.broadcasted_iota(jnp.int32, (blk, blk), 0)
    col = lax.broadcasted_iota(jnp.int32, (blk, blk), 1)
    causal = col <= row
    blk_idx = lax.broadcasted_iota(jnp.int32, (n_blocks, blk), 0)
    pow2 = jnp.left_shift(1, lax.broadcasted_iota(jnp.int32, (n_blocks, LANES), 0)).astype(BF16)
    own = pl.ds(pl.multiple_of(i * blk, blk), blk)

    def select_bits(h):
        q = q_ref[:, _head_slice(h)]
        km = kmean_ref[h]
        km_hi = km.astype(BF16)
        km_lo = (km - km_hi.astype(F32)).astype(BF16)
        gate = _dot_nt(km_hi, q) + _dot_nt(km_lo, q)
        rank = jnp.zeros(gate.shape, jnp.int32)
        for m in range(n_blocks):
            gm = gate[m:m + 1, :]
            beats = (gm > gate) | ((gm == gate) & (m < blk_idx))
            rank = rank + jnp.where(beats & (m < i), 1, 0)
        selected = (blk_idx < i) & (rank < MOBA_TOPK)
        return _dot_tn(selected.astype(BF16), pow2).astype(jnp.int32)

    bits = [select_bits(h) for h in range(heads)]

    states = []
    for h in range(heads):
        hs = _head_slice(h)
        s = jnp.where(causal, _dot_nt(q_ref[:, hs], k_ref[own, hs]), MASK_VALUE)
        states.append(_softmax_init(s, vaug_ref[h, own, :]))

    def body(c, states):
        rows = pl.ds(pl.multiple_of(c * (per_iter * blk), per_iter * blk), per_iter * blk)
        new_states = []
        for h in range(heads):
            hs = _head_slice(h)
            s = _dot_nt(q_ref[:, hs], k_ref[rows, hs])
            pieces = []
            for t in range(per_iter):
                keep = jnp.left_shift(bits[h], 31 - (c * per_iter + t)) < 0
                for lt in range(blk // LANES):
                    lo = t * blk + lt * LANES
                    pieces.append(jnp.where(keep, s[:, lo:lo + LANES], MASK_VALUE))
            s = jnp.concatenate(pieces, axis=1)
            new_states.append(_softmax_update(states[h], s, vaug_ref[h, rows, :]))
        return tuple(new_states)

    n_iters = (i + per_iter - 1) // per_iter
    states = lax.fori_loop(0, n_iters, body, tuple(states))
    for h in range(heads):
        o_ref[:, _head_slice(h)] = _softmax_finish(states[h]).astype(o_ref.dtype)


def moba_attention(qkv, bsz, seq, n_heads, q_head0, k_head0, v_head0, n_out_heads):
    nb = seq // MOBA_BLOCK
    hp = MOBA_HEADS_PER_STEP
    assert n_heads % hp == 0 and q_head0 % hp == 0 and k_head0 % hp == 0 and v_head0 % hp == 0
    assert nb % MOBA_BLOCKS_PER_ITER == 0
    w = hp * HEAD_DIM
    return pl.pallas_call(
        functools.partial(_moba_kernel, n_blocks=nb, heads=hp),
        out_shape=jax.ShapeDtypeStruct((bsz * seq, n_out_heads * HEAD_DIM), BF16),
        grid=(bsz, n_heads // hp, nb),
        in_specs=[pl.BlockSpec((MOBA_BLOCK, w), lambda b, h, i: (b * nb + i, q_head0 // hp + h)),
                  pl.BlockSpec((seq, w), lambda b, h, i: (b, k_head0 // hp + h)),
                  pl.BlockSpec((seq, w), lambda b, h, i: (b, v_head0 // hp + h))],
        out_specs=pl.BlockSpec((MOBA_BLOCK, w), lambda b, h, i: (b * nb + i, h)),
        scratch_shapes=[pltpu.VMEM((hp, nb, HEAD_DIM), F32),
                        pltpu.VMEM((hp, seq, 2 * HEAD_DIM), BF16)],
        compiler_params=_params(("parallel", "parallel", "arbitrary")),
        name="moba_attention",
    )(qkv, qkv, qkv)


DIL_TILE = 256
DIL_BACK = 128


def _residue_permutation(dil):
    per = DIL_TILE // dil
    r = lax.broadcasted_iota(jnp.int32, (DIL_TILE, DIL_TILE), 0)
    c = lax.broadcasted_iota(jnp.int32, (DIL_TILE, DIL_TILE), 1)
    return (c == (r % per) * dil + r // per).astype(BF16)


def _band_tile(q, k, vaug, q0, k0):
    n = k.shape[0]
    s = _dot_nt(q, k)
    dist = (q0 - k0) + (lax.broadcasted_iota(jnp.int32, (DIL_TILE, n), 0)
                        - lax.broadcasted_iota(jnp.int32, (DIL_TILE, n), 1))
    s = jnp.where((dist >= 0) & (dist <= DIL_BACK), s, MASK_VALUE)
    m = jnp.max(s, axis=1, keepdims=True)
    p = jnp.exp2(s - m)
    ol = _dot(p.astype(vaug.dtype), vaug)
    l = ol[:, HEAD_DIM:]
    return ol[:, :HEAD_DIM] / l, m + jnp.log(l) * LOG2_E


def _dilated_kernel(q0_ref, q1_ref, q2_ref, k0_ref, k1_ref, k2_ref, v0_ref, v1_ref, v2_ref, _buf_ref,
                    o_ref, v0aug_ref, qs1_ref, ks1_ref, vs1_ref, qs2_ref, ks2_ref, vs2_ref, out_ref, lse_ref, *, seq):
    tile = DIL_TILE
    n_tiles = seq // tile
    keys = tile + DIL_BACK
    groups = ((q1_ref, k1_ref, v1_ref, qs1_ref, ks1_ref, vs1_ref),
              (q2_ref, k2_ref, v2_ref, qs2_ref, ks2_ref, vs2_ref))

    v0aug_ref[...] = _ones_augmented(v0_ref[...])
    for (_, dil), refs in zip(B_PATTERNS[1:], groups):
        per = tile // dil
        perm = _residue_permutation(dil)

        def gather(b, carry, per=per, perm=perm, refs=refs, dil=dil):
            q_ref, k_ref, v_ref, qs_ref, ks_ref, vs_ref = refs
            rows = pl.ds(pl.multiple_of(b * tile, tile), tile)
            dst = pl.ds(pl.multiple_of(b * per, per), per)
            qkv = jnp.concatenate([q_ref[rows, :], k_ref[rows, :], v_ref[rows, :]], axis=1)
            moved = _dot(perm, qkv).astype(BF16)
            qs_ref[:, dst, :] = moved[:, _head_slice(0)].reshape(dil, per, HEAD_DIM)
            ks_ref[:, dst, :] = moved[:, _head_slice(1)].reshape(dil, per, HEAD_DIM)
            vs_ref[:, dst, :] = _ones_augmented(moved[:, _head_slice(2)]).reshape(dil, per, 2 * HEAD_DIM)
            return carry

        lax.fori_loop(0, n_tiles, gather, 0)

    def window(t0):
        k0 = jnp.maximum(t0 - DIL_BACK, 0)
        return k0, pl.ds(pl.multiple_of(k0, DIL_BACK), keys)

    def tile_g0(i, carry):
        t0 = pl.multiple_of(i * tile, tile)
        k0, krows = window(t0)
        out, lse = _band_tile(q0_ref[pl.ds(t0, tile), :], k0_ref[krows, :], v0aug_ref[krows, :], t0, k0)
        out_ref[0, pl.ds(t0, tile), :] = out
        lse_ref[0, pl.ds(t0, tile), :] = lse
        return carry

    lax.fori_loop(0, n_tiles, tile_g0, 0, unroll=4)

    for g, ((_, dil), refs) in enumerate(zip(B_PATTERNS[1:], groups), start=1):
        sub_tiles = n_tiles // dil

        def tile_g(it, carry, g=g, dil=dil, sub_tiles=sub_tiles, refs=refs[3:]):
            qs_ref, ks_ref, vs_ref = refs
            res, ti = it // sub_tiles, it % sub_tiles
            t0 = pl.multiple_of(ti * tile, tile)
            if sub_tiles == 1:
                k0, krows = 0, pl.ds(0, tile)
            else:
                k0, krows = window(t0)
            out, lse = _band_tile(qs_ref[res, pl.ds(t0, tile), :], ks_ref[res, krows, :],
                                  vs_ref[res, krows, :], t0, k0)
            natural = pl.ds(t0 * dil + res, tile, stride=dil)
            out_ref[g, natural, :] = out
            lse_ref[g, natural, :] = lse
            return carry

        lax.fori_loop(0, n_tiles, tile_g, 0, unroll=4)

    def merge(i, carry):
        rows = pl.ds(pl.multiple_of(i * tile, tile), tile)
        lses = [lse_ref[g, rows, :] for g in range(len(B_PATTERNS))]
        top = functools.reduce(jnp.maximum, lses)
        ws = [jnp.exp2(lse - top) for lse in lses]
        num = sum(w * out_ref[g, rows, :] for g, w in enumerate(ws))
        o_ref[rows, :] = (num / sum(ws)).astype(o_ref.dtype)
        return carry

    lax.fori_loop(0, n_tiles, merge, 0)


def dilated_attention(qkv, out_buf, bsz, seq, q_head0, k_head0, v_head0, out_head0):
    hg = N_HEADS_B_GROUP
    ng = len(B_PATTERNS)
    tile = DIL_TILE
    assert all(w // d == DIL_BACK and tile % d == 0 and (seq // tile) % d == 0 for w, d in B_PATTERNS)
    assert B_PATTERNS[0][1] == 1 and seq % tile == 0 and seq >= tile + DIL_BACK

    def spec(head0, g):
        return pl.BlockSpec((seq, HEAD_DIM), lambda b, h, g=g: (b, head0 + g * hg + h))

    scratch = [pltpu.VMEM((seq, 2 * HEAD_DIM), BF16)]
    for _, dil in B_PATTERNS[1:]:
        scratch += [pltpu.VMEM((dil, seq // dil, HEAD_DIM), BF16), pltpu.VMEM((dil, seq // dil, HEAD_DIM), BF16),
                    pltpu.VMEM((dil, seq // dil, 2 * HEAD_DIM), BF16)]
    scratch += [pltpu.VMEM((ng, seq, HEAD_DIM), F32), pltpu.VMEM((ng, seq, HEAD_DIM), F32)]
    return pl.pallas_call(
        functools.partial(_dilated_kernel, seq=seq),
        out_shape=jax.ShapeDtypeStruct(out_buf.shape, out_buf.dtype),
        grid=(bsz, hg),
        in_specs=([spec(q_head0, g) for g in range(ng)] + [spec(k_head0, g) for g in range(ng)]
                  + [spec(v_head0, g) for g in range(ng)] + [pl.BlockSpec(memory_space=pl.ANY)]),
        out_specs=pl.BlockSpec((seq, HEAD_DIM), lambda b, h: (b, out_head0 + h)),
        scratch_shapes=scratch,
        input_output_aliases={3 * ng: 0},
        compiler_params=_params(("parallel", "parallel")),
        name="dilated_attention",
    )(*([qkv] * (3 * ng)), out_buf)


def _sgu_kernel(u_ref, v_ref, gain_ref, bias_ref, ws_ref, bs_ref, o_ref):
    v = v_ref[...].astype(F32)
    mu = jnp.mean(v, axis=-1, keepdims=True)
    vc = v - mu
    vn = vc * lax.rsqrt(jnp.mean(vc * vc, axis=-1, keepdims=True) + NORM_EPS)
    vn = (vn * gain_ref[...] + bias_ref[...]).astype(BF16)
    c = GMLP_CHUNK
    row = lax.broadcasted_iota(jnp.int32, (c, c), 0)
    col = lax.broadcasted_iota(jnp.int32, (c, c), 1)
    causal = col <= row
    for g in range(ws_ref.shape[0]):
        sl = slice(g * GMLP_GROUP_DIM, (g + 1) * GMLP_GROUP_DIM)
        w = jnp.where(causal, ws_ref[g], 0.0).astype(BF16)
        f = _dot(w, vn[:, sl]) + bs_ref[:, g:g + 1]
        o_ref[:, sl] = (u_ref[:, sl].astype(F32) * f).astype(o_ref.dtype)


def spatial_gating(z, gain, bias, w_s, b_s):
    t, e2 = z.shape
    e = e2 // 2
    groups = w_s.shape[0]
    c = GMLP_CHUNK
    return pl.pallas_call(
        _sgu_kernel,
        out_shape=jax.ShapeDtypeStruct((t, e), BF16),
        grid=(t // c,),
        in_specs=[pl.BlockSpec((c, e), lambda i: (i, 0)),
                  pl.BlockSpec((c, e), lambda i: (i, 1)),
                  pl.BlockSpec((1, e), lambda i: (0, 0)),
                  pl.BlockSpec((1, e), lambda i: (0, 0)),
                  pl.BlockSpec((groups, c, c), lambda i: (0, 0, 0)),
                  pl.BlockSpec((c, groups), lambda i: (0, 0))],
        out_specs=pl.BlockSpec((c, e), lambda i: (i, 0)),
        compiler_params=_params(("parallel",)),
        name="spatial_gating",
    )(z, z, gain.reshape(1, e), bias.reshape(1, e), w_s, b_s.T)


def _causal_conv(a, prev, cw, cb):
    tm, n = a.shape
    a3 = a.reshape(tm // SUBLANES, SUBLANES, n)
    row_in_group = lax.broadcasted_iota(jnp.int32, (1, SUBLANES, n), 1)
    out = cb + a * cw[CONV_WIDTH - 1:CONV_WIDTH]
    for back in range(1, CONV_WIDTH):
        rot = pltpu.roll(a3, back, axis=1)
        before = jnp.concatenate([pltpu.roll(prev, back, axis=0)[None], rot[:-1]], axis=0)
        shifted = jnp.where(row_in_group < back, before, rot).reshape(tm, n)
        out = out + shifted * cw[CONV_WIDTH - 1 - back:CONV_WIDTH - back]
    return out


def _streamed_weight_spec(layer, k, tn, ni, nj, col0):
    assert k % ni == 0
    return pl.BlockSpec((None, k // ni, tn), lambda j, i: (layer, i, col0 + jnp.minimum(j, nj - 1)))


def _streamed_out_spec(tm, tn):
    return pl.BlockSpec((tm, tn), lambda j, i: (jnp.where(j == 0, 0, i), jnp.maximum(j - 1, 0)))


def _stream_weight_rows(*refs):
    j, i = pl.program_id(0), pl.program_id(1)
    fill = j % 2
    for w32_ref, w_ref in zip(refs[0::2], refs[1::2]):
        chunk = w32_ref.shape[0]
        w_ref[fill, pl.ds(pl.multiple_of(i * chunk, chunk), chunk), :] = w32_ref[...].astype(w_ref.dtype)
    return j - 1, 1 - fill


def _ffn_up_kernel(h_ref, wg32_ref, wu32_ref, cwg_ref, cwu_ref, cbg_ref, cbu_ref, o_ref, pg_ref, pu_ref,
                   wg_ref, wu_ref, *, tiles_per_seq, chunk, rows):
    i = pl.program_id(1)
    tm, tn = o_ref.shape
    col, cur = _stream_weight_rows(wg32_ref, wg_ref, wu32_ref, wu_ref)

    @pl.when(i % tiles_per_seq == 0)
    def _():
        pg_ref[...] = jnp.zeros(pg_ref.shape, F32)
        pu_ref[...] = jnp.zeros(pu_ref.shape, F32)

    @pl.when(col >= 0)
    def _():
        for c0 in range(0, tn, chunk):
            sl = slice(c0, c0 + chunk)
            prev_g, prev_u = pg_ref[:, sl], pu_ref[:, sl]
            for r0 in range(0, tm, rows):
                rs = slice(r0, r0 + rows)
                ag = _dot(h_ref[rs, :], wg_ref[cur, :, sl])
                au = _dot(h_ref[rs, :], wu_ref[cur, :, sl])
                gate = _causal_conv(ag, prev_g, cwg_ref[:, sl], cbg_ref[:, sl])
                up = _causal_conv(au, prev_u, cwu_ref[:, sl], cbu_ref[:, sl])
                o_ref[rs, sl] = (jax.nn.silu(gate) * up).astype(o_ref.dtype)
                prev_g, prev_u = ag[rows - SUBLANES:rows], au[rows - SUBLANES:rows]
            pg_ref[:, sl] = prev_g
            pu_ref[:, sl] = prev_u


def ffn_up(h, w_up, layer, conv_w, conv_b, seq, tm=1024, tn=512, chunk=MXU_COLS, rows=512):
    t, k = h.shape
    f = w_up.shape[2] // 2
    tm, tn = min(tm, seq), min(tn, f)
    chunk, rows = min(chunk, tn), min(rows, tm)
    ni, nj = t // tm, f // tn
    assert seq % tm == 0 and f % tn == 0 and tn % chunk == 0 and tm % rows == 0
    cb = conv_b.reshape(1, 2 * f)

    def col_spec(rows_, col0):
        return pl.BlockSpec((rows_, tn), lambda j, i: (0, col0 + jnp.maximum(j - 1, 0)))

    return pl.pallas_call(
        functools.partial(_ffn_up_kernel, tiles_per_seq=seq // tm, chunk=chunk, rows=rows),
        out_shape=jax.ShapeDtypeStruct((t, f), BF16),
        grid=(nj + 1, ni),
        in_specs=[pl.BlockSpec((tm, k), lambda j, i: (i, 0)),
                  _streamed_weight_spec(layer, k, tn, ni, nj, 0),
                  _streamed_weight_spec(layer, k, tn, ni, nj, nj),
                  col_spec(CONV_WIDTH, 0), col_spec(CONV_WIDTH, nj),
                  col_spec(1, 0), col_spec(1, nj)],
        out_specs=_streamed_out_spec(tm, tn),
        scratch_shapes=[pltpu.VMEM((SUBLANES, tn), F32), pltpu.VMEM((SUBLANES, tn), F32),
                        pltpu.VMEM((2, k, tn), BF16), pltpu.VMEM((2, k, tn), BF16)],
        compiler_params=_params(("arbitrary", "arbitrary")),
        name="ffn_up_conv_act",
    )(h, w_up, w_up, conv_w, conv_w, cb, cb)


def conv_ffn_block(x, norm_g, w_up, w_down, layer, conv_w, conv_b, seq):
    h = rmsnorm(x, norm_g, BF16)
    act = ffn_up(h, w_up, layer, conv_w, conv_b, seq)
    return matmul_residual(act, w_down, layer, x, tm=1024, tn=256, a_buffers=1)


def hybrid_attention_block(x, positions, norm_g, w_in, w_out, layer, bsz, seq):
    h = rmsnorm(x, norm_g, BF16)
    cos, sin_lo, sin_hi = rotary_tables(positions)
    nh = N_HEADS_QKV
    qkv = qkv_projection(h, w_in, layer, cos, sin_lo, sin_hi,
                         n_q_cols=nh * HEAD_DIM, n_rot_cols=2 * nh * HEAD_DIM)
    o = moba_attention(qkv, bsz, seq, N_HEADS_A, 0, nh, 2 * nh, N_HEADS_A + N_HEADS_B_GROUP)
    o = dilated_attention(qkv, o, bsz, seq, N_HEADS_A, nh + N_HEADS_A, 2 * nh + N_HEADS_A, N_HEADS_A)
    return matmul_residual(o, w_out, layer, x, tm=1024, tn=1024)


def spatial_gating_block(x, norm_g, w_in, w_out, layer, b_in, v_gain, v_bias, w_s, b_s):
    h = rmsnorm(x, norm_g, BF16)
    z = matmul_bias_gelu(h, w_in, layer, b_in)
    gated = spatial_gating(z, v_gain, v_bias, w_s, b_s)
    return matmul_residual(gated, w_out, layer, x, tm=1024, tn=1024)


def kernel(x, positions, attn_norm, attn_w_in, attn_w_out, sg_norm, sg_w_in, sg_b_in, sg_v_gain,
           sg_v_bias, sg_w_s, sg_b_s, sg_w_out, ffn_norm, ffn_w_up, ffn_conv_w, ffn_conv_b,
           ffn_w_down, final_norm):
    bsz, seq, d = x.shape
    depth = ffn_norm.shape[0]
    attn_w_out, sg_w_out, ffn_w_down = (w.astype(BF16) for w in (attn_w_out, sg_w_out, ffn_w_down))
    h = x.reshape(bsz * seq, d)
    for layer in range(depth):
        i = layer // 2
        if layer % 2 == 0:
            h = hybrid_attention_block(h, positions, attn_norm[i], attn_w_in, attn_w_out, i, bsz, seq)
        else:
            h = spatial_gating_block(h, sg_norm[i], sg_w_in, sg_w_out, i, sg_b_in[i], sg_v_gain[i],
                                     sg_v_bias[i], sg_w_s[i], sg_b_s[i])
        h = conv_ffn_block(h, ffn_norm[layer], ffn_w_up, ffn_w_down, layer, ffn_conv_w[layer],
                           ffn_conv_b[layer], seq)
    return rmsnorm(h, final_norm, x.dtype).reshape(bsz, seq, d)
```

```python
import functools
import math

import jax
import jax.numpy as jnp
from jax import lax
from jax.experimental import pallas as pl
from jax.experimental.pallas import tpu as pltpu

F32 = jnp.float32
BF16 = jnp.bfloat16

HEAD_DIM = 128
ROT_DIM = HEAD_DIM // 4
ROPE_THETA = 500000.0
N_HEADS_A = 24
N_HEADS_B_GROUP = 8
B_PATTERNS = ((128, 1), (512, 4), (2048, 16))
N_HEADS_QKV = N_HEADS_A + N_HEADS_B_GROUP * len(B_PATTERNS)
MOBA_BLOCK = 256
MOBA_TOPK = 3
GMLP_CHUNK = 128
GMLP_GROUP_DIM = 128
CONV_WIDTH = 3
NORM_EPS = 1e-5

LANES = 128
SUBLANES = 8
MXU_COLS = 256
VMEM_LIMIT_BYTES = 56 * 1024 * 1024

MASK_VALUE = -1e30
LOG2_E = math.log2(math.e)
Q_SCALE = HEAD_DIM ** -0.5 * LOG2_E


def _params(semantics):
    return pltpu.CompilerParams(dimension_semantics=semantics, vmem_limit_bytes=VMEM_LIMIT_BYTES)


def _dot(a, b):
    return jnp.dot(a, b, preferred_element_type=F32)


def _dot_nt(a, b):
    return lax.dot_general(a, b, (((1,), (1,)), ((), ())), preferred_element_type=F32)


def _dot_tn(a, b):
    return lax.dot_general(a, b, (((0,), (0,)), ((), ())), preferred_element_type=F32)


def _head_slice(h):
    return slice(h * HEAD_DIM, (h + 1) * HEAD_DIM)


def _rmsnorm_kernel(x_ref, g_ref, o_ref):
    x = x_ref[...]
    y = x * lax.rsqrt(jnp.mean(x * x, axis=-1, keepdims=True) + NORM_EPS)
    o_ref[...] = (y * g_ref[...]).astype(o_ref.dtype)


def rmsnorm(x, g, out_dtype, rows=256):
    t, d = x.shape
    rows = min(rows, t)
    return pl.pallas_call(
        _rmsnorm_kernel,
        out_shape=jax.ShapeDtypeStruct((t, d), out_dtype),
        grid=(t // rows,),
        in_specs=[pl.BlockSpec((rows, d), lambda i: (i, 0)),
                  pl.BlockSpec((1, d), lambda i: (0, 0))],
        out_specs=pl.BlockSpec((rows, d), lambda i: (i, 0)),
        compiler_params=_params(("parallel",)),
        name="rmsnorm",
    )(x, g.reshape(1, d))


def _qkv_kernel(a_ref, w32_ref, cos_ref, sin_lo_ref, sin_hi_ref, o_ref, w_ref, *, n_q_tiles, n_rot_tiles):
    col, cur = _stream_weight_rows(w32_ref, w_ref)
    tn = o_ref.shape[1]

    @pl.when((col >= 0) & (col < n_rot_tiles))
    def _():
        sc = jnp.where(col < n_q_tiles, Q_SCALE, 1.0).astype(F32)
        cos = cos_ref[...] * sc
        sin_lo = sin_lo_ref[...] * sc
        sin_hi = sin_hi_ref[...] * sc
        for c0 in range(0, tn, MXU_COLS):
            x2 = _dot(a_ref[...], w_ref[cur, :, c0:c0 + MXU_COLS])
            for h in range(MXU_COLS // HEAD_DIM):
                x = x2[:, _head_slice(h)]
                up = pltpu.roll(x, HEAD_DIM - ROT_DIM // 2, axis=1)
                down = pltpu.roll(x, ROT_DIM // 2, axis=1)
                lo = c0 + h * HEAD_DIM
                o_ref[:, lo:lo + HEAD_DIM] = (x * cos + up * sin_lo + down * sin_hi).astype(o_ref.dtype)

    @pl.when(col >= n_rot_tiles)
    def _():
        o_ref[...] = _dot(a_ref[...], w_ref[cur]).astype(o_ref.dtype)


def qkv_projection(h, w, layer, cos, sin_lo, sin_hi, n_q_cols, n_rot_cols, tm=1024, tn=1024):
    t, k = h.shape
    n = w.shape[2]
    tm, tn = min(tm, t), min(tn, n)
    ni, nj = t // tm, n // tn
    assert n_q_cols % tn == 0 and n_rot_cols % tn == 0 and tn % MXU_COLS == 0
    return pl.pallas_call(
        functools.partial(_qkv_kernel, n_q_tiles=n_q_cols // tn, n_rot_tiles=n_rot_cols // tn),
        out_shape=jax.ShapeDtypeStruct((t, n), BF16),
        grid=(nj + 1, ni),
        in_specs=[pl.BlockSpec((tm, k), lambda j, i: (i, 0)),
                  _streamed_weight_spec(layer, k, tn, ni, nj, 0),
                  pl.BlockSpec((tm, HEAD_DIM), lambda j, i: (i, 0)),
                  pl.BlockSpec((tm, HEAD_DIM), lambda j, i: (i, 0)),
                  pl.BlockSpec((tm, HEAD_DIM), lambda j, i: (i, 0))],
        out_specs=_streamed_out_spec(tm, tn),
        scratch_shapes=[pltpu.VMEM((2, k, tn), BF16)],
        compiler_params=_params(("arbitrary", "arbitrary")),
        name="qkv_rotary",
    )(h, w, cos, sin_lo, sin_hi)


def rotary_tables(positions):
    half = ROT_DIM // 2
    inv_freq = jnp.power(ROPE_THETA, -jnp.arange(half, dtype=F32) * (2.0 / ROT_DIM))
    ang = positions.reshape(-1).astype(F32)[:, None] * inv_freq
    cos, sin = jnp.cos(ang), jnp.sin(ang)
    t = ang.shape[0]
    ones = jnp.ones((t, HEAD_DIM - ROT_DIM), F32)
    zeros = jnp.zeros((t, HEAD_DIM - ROT_DIM), F32)
    zh = jnp.zeros((t, half), F32)
    cos_t = jnp.concatenate([cos, cos, ones], axis=1)
    sin_lo = jnp.concatenate([-sin, zh, zeros], axis=1)
    sin_hi = jnp.concatenate([zh, sin, zeros], axis=1)
    return cos_t, sin_lo, sin_hi


def _mm_resid_kernel(a_ref, w_ref, r_ref, o_ref):
    o_ref[...] = r_ref[...] + _dot(a_ref[...], w_ref[...])


def matmul_residual(a, w, layer, resid, tm, tn, a_buffers=2):
    t, k = a.shape
    n = w.shape[2]
    tm, tn = min(tm, t), min(tn, n)
    return pl.pallas_call(
        _mm_resid_kernel,
        out_shape=jax.ShapeDtypeStruct((t, n), F32),
        grid=(t // tm, n // tn),
        in_specs=[pl.BlockSpec((tm, k), lambda i, j: (i, 0), pipeline_mode=pl.Buffered(a_buffers)),
                  pl.BlockSpec((None, k, tn), lambda i, j: (layer, 0, j)),
                  pl.BlockSpec((tm, tn), lambda i, j: (i, j))],
        out_specs=pl.BlockSpec((tm, tn), lambda i, j: (i, j)),
        compiler_params=_params(("parallel", "arbitrary")),
        name="matmul_residual",
    )(a, w, resid)


def _mm_bias_gelu_kernel(a_ref, w32_ref, b_ref, o_ref, w_ref):
    col, cur = _stream_weight_rows(w32_ref, w_ref)

    @pl.when(col >= 0)
    def _():
        z = _dot(a_ref[...], w_ref[cur]) + b_ref[...]
        o_ref[...] = (0.5 * z * (1.0 + lax.erf(z * (2.0 ** -0.5)))).astype(o_ref.dtype)


def matmul_bias_gelu(a, w, layer, b, tm=1024, tn=1024):
    t, k = a.shape
    n = w.shape[2]
    tm, tn = min(tm, t), min(tn, n)
    ni, nj = t // tm, n // tn
    return pl.pallas_call(
        _mm_bias_gelu_kernel,
        out_shape=jax.ShapeDtypeStruct((t, n), BF16),
        grid=(nj + 1, ni),
        in_specs=[pl.BlockSpec((tm, k), lambda j, i: (i, 0)),
                  _streamed_weight_spec(layer, k, tn, ni, nj, 0),
                  pl.BlockSpec((1, tn), lambda j, i: (0, jnp.maximum(j - 1, 0)))],
        out_specs=_streamed_out_spec(tm, tn),
        scratch_shapes=[pltpu.VMEM((2, k, tn), BF16)],
        compiler_params=_params(("arbitrary", "arbitrary")),
        name="matmul_bias_gelu",
    )(a, w, b.reshape(1, n))


def _ones_augmented(v):
    return jnp.concatenate([v, jnp.ones_like(v)], axis=1)


def _softmax_init(s, vaug):
    m = jnp.max(s, axis=1, keepdims=True)
    p = jnp.exp2(s - m)
    return m, _dot(p.astype(vaug.dtype), vaug)


def _softmax_update(state, s, vaug):
    m, accl = state
    m_new = jnp.maximum(m, jnp.max(s, axis=1, keepdims=True))
    alpha = jnp.exp2(m - m_new)
    p = jnp.exp2(s - m_new)
    return m_new, alpha * accl + _dot(p.astype(vaug.dtype), vaug)


def _softmax_finish(state):
    _, accl = state
    return accl[:, :HEAD_DIM] / accl[:, HEAD_DIM:]


MOBA_HEADS_PER_STEP = 4
MOBA_BLOCKS_PER_ITER = 2


MOBA_VT_ROWS = HEAD_DIM + 16


def _column_max(x):
    rows, cols = x.shape
    x = x.reshape(rows // SUBLANES, SUBLANES, cols)
    while x.shape[0] > 1:
        half = x.shape[0] // 2
        x = jnp.maximum(x[:half], x[half:])
    return jnp.max(x[0], axis=0, keepdims=True)


def _moba_kernel(q_ref, k_ref, v_ref, o_ref, kmean_ref, vt_ref, *, n_blocks, heads):
    i = pl.program_id(2)
    blk = MOBA_BLOCK
    per_iter = MOBA_BLOCKS_PER_ITER

    @pl.when(i == 0)
    def _():
        for h in range(heads):
            hs = _head_slice(h)
            kf = k_ref[:, hs].astype(F32).reshape(n_blocks, blk, HEAD_DIM)
            kmean_ref[h] = jnp.mean(kf, axis=1)
            for n in range(n_blocks):
                vt = v_ref[n * blk:(n + 1) * blk, hs].astype(F32).T
                vt_ref[h, n, 0:HEAD_DIM, :] = vt.astype(BF16)
                vt_ref[h, n, HEAD_DIM:MOBA_VT_ROWS, :] = jnp.ones((MOBA_VT_ROWS - HEAD_DIM, blk), BF16)

    key_row = lax.broadcasted_iota(jnp.int32, (blk, blk), 0)
    query_col = lax.broadcasted_iota(jnp.int32, (blk, blk), 1)
    causal = key_row <= query_col
    blk_idx = lax.broadcasted_iota(jnp.int32, (n_blocks, blk), 0)
    own = pl.ds(pl.multiple_of(i * blk, blk), blk)

    def gate_scores(q, h):
        km = kmean_ref[h]
        km_hi = km.astype(BF16)
        km_lo = (km - km_hi.astype(F32)).astype(BF16)
        return _dot_nt(km_hi, q) + _dot_nt(km_lo, q)

    def select_bits(gate):
        rank = jnp.zeros(gate.shape, jnp.int32)
        for m in range(n_blocks):
            gm = gate[m:m + 1, :]
            beats = (gm > gate) | ((gm == gate) & (m < blk_idx))
            rank = rank + jnp.where(beats & (m < i), 1, 0)
        selected = (blk_idx < i) & (rank < MOBA_TOPK)
        return jnp.sum(jnp.where(selected, jnp.left_shift(1, blk_idx), 0), axis=0, keepdims=True)

    qs = [q_ref[:, _head_slice(h)] for h in range(heads)]
    gates = [gate_scores(qs[h], h) for h in range(heads)]
    own_scores = [_dot_nt(k_ref[own, _head_slice(h)], qs[h]) for h in range(heads)]
    bits = [select_bits(gates[h]) for h in range(heads)]

    states = []
    for h in range(heads):
        st = jnp.where(causal, own_scores[h], MASK_VALUE)
        m = _column_max(st)
        p = jnp.exp2(st - m).astype(BF16)
        states.append((m, _dot(vt_ref[h, i], p)))

    def body(c, states):
        rows = pl.ds(pl.multiple_of(c * (per_iter * blk), per_iter * blk), per_iter * blk)
        new_states = []
        scores = [_dot_nt(k_ref[rows, _head_slice(h)], qs[h]) for h in range(heads)]
        for h in range(heads):
            m, acc = states[h]
            st = scores[h]
            pieces = []
            for t in range(per_iter):
                keep = jnp.left_shift(bits[h], 31 - (c * per_iter + t)) < 0
                pieces.append(jnp.where(keep, st[t * blk:(t + 1) * blk], MASK_VALUE))
            st = jnp.concatenate(pieces, axis=0)
            m_new = jnp.maximum(m, _column_max(st))
            p = jnp.exp2(st - m_new).astype(BF16)
            acc = jnp.exp2(m - m_new) * acc
            for t in range(per_iter):
                acc = acc + _dot(vt_ref[h, c * per_iter + t], p[t * blk:(t + 1) * blk])
            new_states.append((m_new, acc))
        return tuple(new_states)

    n_iters = (i + per_iter - 1) // per_iter
    states = lax.fori_loop(0, n_iters, body, tuple(states))
    for h in range(heads):
        _, acc = states[h]
        out_t = acc[:HEAD_DIM] / acc[HEAD_DIM:HEAD_DIM + 1]
        o_ref[:, _head_slice(h)] = out_t.T.astype(o_ref.dtype)


def moba_attention(qkv, bsz, seq, n_heads, q_head0, k_head0, v_head0, n_out_heads):
    nb = seq // MOBA_BLOCK
    hp = MOBA_HEADS_PER_STEP
    assert n_heads % hp == 0 and q_head0 % hp == 0 and k_head0 % hp == 0 and v_head0 % hp == 0
    assert nb % MOBA_BLOCKS_PER_ITER == 0
    w = hp * HEAD_DIM
    return pl.pallas_call(
        functools.partial(_moba_kernel, n_blocks=nb, heads=hp),
        out_shape=jax.ShapeDtypeStruct((bsz * seq, n_out_heads * HEAD_DIM), BF16),
        grid=(bsz, n_heads // hp, nb),
        in_specs=[pl.BlockSpec((MOBA_BLOCK, w), lambda b, h, i: (b * nb + i, q_head0 // hp + h)),
                  pl.BlockSpec((seq, w), lambda b, h, i: (b, k_head0 // hp + h)),
                  pl.BlockSpec((seq, w), lambda b, h, i: (b, v_head0 // hp + h))],
        out_specs=pl.BlockSpec((MOBA_BLOCK, w), lambda b, h, i: (b * nb + i, h)),
        scratch_shapes=[pltpu.VMEM((hp, nb, HEAD_DIM), F32),
                        pltpu.VMEM((hp, nb, MOBA_VT_ROWS, MOBA_BLOCK), BF16)],
        compiler_params=_params(("parallel", "parallel", "arbitrary")),
        name="moba_attention",
    )(qkv, qkv, qkv)


DIL_TILE = 256
DIL_BACK = 128


def _residue_permutation(dil):
    per = DIL_TILE // dil
    r = lax.broadcasted_iota(jnp.int32, (DIL_TILE, DIL_TILE), 0)
    c = lax.broadcasted_iota(jnp.int32, (DIL_TILE, DIL_TILE), 1)
    return (c == (r % per) * dil + r // per).astype(BF16)


DIL_TILES_PER_ITER = 4


def _band_tiles(tiles):
    scores = [_dot_nt(q, k) for q, k, _, _, _ in tiles]
    results = []
    for s, (_, k, vaug, q0, k0) in zip(scores, tiles):
        n = k.shape[0]
        dist = (q0 - k0) + (lax.broadcasted_iota(jnp.int32, (DIL_TILE, n), 0)
                            - lax.broadcasted_iota(jnp.int32, (DIL_TILE, n), 1))
        s = jnp.where((dist >= 0) & (dist <= DIL_BACK), s, MASK_VALUE)
        m = jnp.max(s, axis=1, keepdims=True)
        p = jnp.exp2(s - m)
        ol = _dot(p.astype(vaug.dtype), vaug)
        l = ol[:, HEAD_DIM:]
        results.append((ol[:, :HEAD_DIM] / l, m + jnp.log(l) * LOG2_E))
    return results


def _dilated_kernel(q0_ref, q1_ref, q2_ref, k0_ref, k1_ref, k2_ref, v0_ref, v1_ref, v2_ref, _buf_ref,
                    o_ref, v0aug_ref, qs1_ref, ks1_ref, vs1_ref, qs2_ref, ks2_ref, vs2_ref, out_ref, lse_ref, *, seq):
    tile = DIL_TILE
    n_tiles = seq // tile
    keys = tile + DIL_BACK
    groups = ((q1_ref, k1_ref, v1_ref, qs1_ref, ks1_ref, vs1_ref),
              (q2_ref, k2_ref, v2_ref, qs2_ref, ks2_ref, vs2_ref))

    v0aug_ref[...] = _ones_augmented(v0_ref[...])
    for (_, dil), refs in zip(B_PATTERNS[1:], groups):
        per = tile // dil
        perm = _residue_permutation(dil)

        def gather(b, carry, per=per, perm=perm, refs=refs, dil=dil):
            q_ref, k_ref, v_ref, qs_ref, ks_ref, vs_ref = refs
            rows = pl.ds(pl.multiple_of(b * tile, tile), tile)
            dst = pl.ds(pl.multiple_of(b * per, per), per)
            qkv = jnp.concatenate([q_ref[rows, :], k_ref[rows, :], v_ref[rows, :]], axis=1)
            moved = _dot(perm, qkv).astype(BF16)
            qs_ref[:, dst, :] = moved[:, _head_slice(0)].reshape(dil, per, HEAD_DIM)
            ks_ref[:, dst, :] = moved[:, _head_slice(1)].reshape(dil, per, HEAD_DIM)
            vs_ref[:, dst, :] = _ones_augmented(moved[:, _head_slice(2)]).reshape(dil, per, 2 * HEAD_DIM)
            return carry

        lax.fori_loop(0, n_tiles, gather, 0)

    def window(t0):
        k0 = jnp.maximum(t0 - DIL_BACK, 0)
        return k0, pl.ds(pl.multiple_of(k0, DIL_BACK), keys)

    per_iter = DIL_TILES_PER_ITER

    def tiles_g0(it, carry):
        tiles, dsts = [], []
        for u in range(per_iter):
            t0 = pl.multiple_of((it * per_iter + u) * tile, tile)
            k0, krows = window(t0)
            tiles.append((q0_ref[pl.ds(t0, tile), :], k0_ref[krows, :], v0aug_ref[krows, :], t0, k0))
            dsts.append(pl.ds(t0, tile))
        for dst, (out, lse) in zip(dsts, _band_tiles(tiles)):
            out_ref[0, dst, :] = out
            lse_ref[0, dst, :] = lse
        return carry

    lax.fori_loop(0, n_tiles // per_iter, tiles_g0, 0)

    for g, ((_, dil), refs) in enumerate(zip(B_PATTERNS[1:], groups), start=1):
        sub_tiles = n_tiles // dil

        def tiles_g(it, carry, g=g, dil=dil, sub_tiles=sub_tiles, refs=refs[3:]):
            qs_ref, ks_ref, vs_ref = refs
            tiles, dsts = [], []
            for u in range(per_iter):
                n = it * per_iter + u
                res, ti = n // sub_tiles, n % sub_tiles
                t0 = pl.multiple_of(ti * tile, tile)
                if sub_tiles == 1:
                    k0, krows = 0, pl.ds(0, tile)
                else:
                    k0, krows = window(t0)
                tiles.append((qs_ref[res, pl.ds(t0, tile), :], ks_ref[res, krows, :], vs_ref[res, krows, :],
                              t0, k0))
                dsts.append(pl.ds(t0 * dil + res, tile, stride=dil))
            for dst, (out, lse) in zip(dsts, _band_tiles(tiles)):
                out_ref[g, dst, :] = out
                lse_ref[g, dst, :] = lse
            return carry

        lax.fori_loop(0, n_tiles // per_iter, tiles_g, 0)

    def merge(i, carry):
        rows = pl.ds(pl.multiple_of(i * tile, tile), tile)
        lses = [lse_ref[g, rows, :] for g in range(len(B_PATTERNS))]
        top = functools.reduce(jnp.maximum, lses)
        ws = [jnp.exp2(lse - top) for lse in lses]
        num = sum(w * out_ref[g, rows, :] for g, w in enumerate(ws))
        o_ref[rows, :] = (num / sum(ws)).astype(o_ref.dtype)
        return carry

    lax.fori_loop(0, n_tiles, merge, 0)


def dilated_attention(qkv, out_buf, bsz, seq, q_head0, k_head0, v_head0, out_head0):
    hg = N_HEADS_B_GROUP
    ng = len(B_PATTERNS)
    tile = DIL_TILE
    assert all(w // d == DIL_BACK and tile % d == 0 and (seq // tile) % d == 0 for w, d in B_PATTERNS)
    assert B_PATTERNS[0][1] == 1 and seq % tile == 0 and seq >= tile + DIL_BACK

    def spec(head0, g):
        return pl.BlockSpec((seq, HEAD_DIM), lambda b, h, g=g: (b, head0 + g * hg + h))

    scratch = [pltpu.VMEM((seq, 2 * HEAD_DIM), BF16)]
    for _, dil in B_PATTERNS[1:]:
        scratch += [pltpu.VMEM((dil, seq // dil, HEAD_DIM), BF16), pltpu.VMEM((dil, seq // dil, HEAD_DIM), BF16),
                    pltpu.VMEM((dil, seq // dil, 2 * HEAD_DIM), BF16)]
    scratch += [pltpu.VMEM((ng, seq, HEAD_DIM), F32), pltpu.VMEM((ng, seq, HEAD_DIM), F32)]
    return pl.pallas_call(
        functools.partial(_dilated_kernel, seq=seq),
        out_shape=jax.ShapeDtypeStruct(out_buf.shape, out_buf.dtype),
        grid=(bsz, hg),
        in_specs=([spec(q_head0, g) for g in range(ng)] + [spec(k_head0, g) for g in range(ng)]
                  + [spec(v_head0, g) for g in range(ng)] + [pl.BlockSpec(memory_space=pl.ANY)]),
        out_specs=pl.BlockSpec((seq, HEAD_DIM), lambda b, h: (b, out_head0 + h)),
        scratch_shapes=scratch,
        input_output_aliases={3 * ng: 0},
        compiler_params=_params(("parallel", "parallel")),
        name="dilated_attention",
    )(*([qkv] * (3 * ng)), out_buf)


def _sgu_kernel(u_ref, v_ref, gain_ref, bias_ref, ws_ref, bs_ref, o_ref):
    v = v_ref[...].astype(F32)
    mu = jnp.mean(v, axis=-1, keepdims=True)
    vc = v - mu
    vn = vc * lax.rsqrt(jnp.mean(vc * vc, axis=-1, keepdims=True) + NORM_EPS)
    vn = (vn * gain_ref[...] + bias_ref[...]).astype(BF16)
    c = GMLP_CHUNK
    row = lax.broadcasted_iota(jnp.int32, (c, c), 0)
    col = lax.broadcasted_iota(jnp.int32, (c, c), 1)
    causal = col <= row
    for g in range(ws_ref.shape[0]):
        sl = slice(g * GMLP_GROUP_DIM, (g + 1) * GMLP_GROUP_DIM)
        w = jnp.where(causal, ws_ref[g], 0.0).astype(BF16)
        f = _dot(w, vn[:, sl]) + bs_ref[:, g:g + 1]
        o_ref[:, sl] = (u_ref[:, sl].astype(F32) * f).astype(o_ref.dtype)


def spatial_gating(z, gain, bias, w_s, b_s):
    t, e2 = z.shape
    e = e2 // 2
    groups = w_s.shape[0]
    c = GMLP_CHUNK
    return pl.pallas_call(
        _sgu_kernel,
        out_shape=jax.ShapeDtypeStruct((t, e), BF16),
        grid=(t // c,),
        in_specs=[pl.BlockSpec((c, e), lambda i: (i, 0)),
                  pl.BlockSpec((c, e), lambda i: (i, 1)),
                  pl.BlockSpec((1, e), lambda i: (0, 0)),
                  pl.BlockSpec((1, e), lambda i: (0, 0)),
                  pl.BlockSpec((groups, c, c), lambda i: (0, 0, 0)),
                  pl.BlockSpec((c, groups), lambda i: (0, 0))],
        out_specs=pl.BlockSpec((c, e), lambda i: (i, 0)),
        compiler_params=_params(("parallel",)),
        name="spatial_gating",
    )(z, z, gain.reshape(1, e), bias.reshape(1, e), w_s, b_s.T)


def _causal_conv(a, prev, cw, cb):
    tm, n = a.shape
    a3 = a.reshape(tm // SUBLANES, SUBLANES, n)
    row_in_group = lax.broadcasted_iota(jnp.int32, (1, SUBLANES, n), 1)
    out = cb + a * cw[CONV_WIDTH - 1:CONV_WIDTH]
    for back in range(1, CONV_WIDTH):
        rot = pltpu.roll(a3, back, axis=1)
        before = jnp.concatenate([pltpu.roll(prev, back, axis=0)[None], rot[:-1]], axis=0)
        shifted = jnp.where(row_in_group < back, before, rot).reshape(tm, n)
        out = out + shifted * cw[CONV_WIDTH - 1 - back:CONV_WIDTH - back]
    return out


def _streamed_weight_spec(layer, k, tn, ni, nj, col0):
    assert k % ni == 0
    return pl.BlockSpec((None, k // ni, tn), lambda j, i: (layer, i, col0 + jnp.minimum(j, nj - 1)))


def _streamed_out_spec(tm, tn):
    return pl.BlockSpec((tm, tn), lambda j, i: (jnp.where(j == 0, 0, i), jnp.maximum(j - 1, 0)))


def _stream_weight_rows(*refs):
    j, i = pl.program_id(0), pl.program_id(1)
    fill = j % 2
    for w32_ref, w_ref in zip(refs[0::2], refs[1::2]):
        chunk = w32_ref.shape[0]
        w_ref[fill, pl.ds(pl.multiple_of(i * chunk, chunk), chunk), :] = w32_ref[...].astype(w_ref.dtype)
    return j - 1, 1 - fill


def _ffn_up_kernel(h_ref, wg32_ref, wu32_ref, cwg_ref, cwu_ref, cbg_ref, cbu_ref, o_ref, pg_ref, pu_ref,
                   wg_ref, wu_ref, ag_ref, au_ref, *, tiles_per_seq, chunk, rows):
    i = pl.program_id(1)
    tm, tn = o_ref.shape
    col, cur = _stream_weight_rows(wg32_ref, wg_ref, wu32_ref, wu_ref)

    @pl.when(i % tiles_per_seq == 0)
    def _():
        pg_ref[...] = jnp.zeros(pg_ref.shape, F32)
        pu_ref[...] = jnp.zeros(pu_ref.shape, F32)

    @pl.when(col >= 0)
    def _():
        tiles = [(c0, r0) for c0 in range(0, tn, chunk) for r0 in range(0, tm, rows)]

        def matmuls(n):
            c0, r0 = tiles[n]
            sl, rs = slice(c0, c0 + chunk), slice(r0, r0 + rows)
            ag_ref[(n + cur) % 2] = _dot(h_ref[rs, :], wg_ref[cur, :, sl])
            au_ref[(n + cur) % 2] = _dot(h_ref[rs, :], wu_ref[cur, :, sl])

        matmuls(0)
        prev_g = prev_u = None
        for n, (c0, r0) in enumerate(tiles):
            sl, rs = slice(c0, c0 + chunk), slice(r0, r0 + rows)
            if n + 1 < len(tiles):
                matmuls(n + 1)
            if r0 == 0:
                prev_g, prev_u = pg_ref[:, sl], pu_ref[:, sl]
            ag, au = ag_ref[(n + cur) % 2], au_ref[(n + cur) % 2]
            gate = _causal_conv(ag, prev_g, cwg_ref[:, sl], cbg_ref[:, sl])
            up = _causal_conv(au, prev_u, cwu_ref[:, sl], cbu_ref[:, sl])
            o_ref[rs, sl] = (jax.nn.silu(gate) * up).astype(o_ref.dtype)
            prev_g, prev_u = ag[rows - SUBLANES:rows], au[rows - SUBLANES:rows]
            if r0 + rows == tm:
                pg_ref[:, sl] = prev_g
                pu_ref[:, sl] = prev_u


def ffn_up(h, w_up, layer, conv_w, conv_b, seq, tm=1024, tn=512, chunk=MXU_COLS, rows=128):
    t, k = h.shape
    f = w_up.shape[2] // 2
    tm, tn = min(tm, seq), min(tn, f)
    chunk, rows = min(chunk, tn), min(rows, tm)
    ni, nj = t // tm, f // tn
    assert seq % tm == 0 and f % tn == 0 and tn % chunk == 0 and tm % rows == 0
    cb = conv_b.reshape(1, 2 * f)

    def col_spec(rows_, col0):
        return pl.BlockSpec((rows_, tn), lambda j, i: (0, col0 + jnp.maximum(j - 1, 0)))

    return pl.pallas_call(
        functools.partial(_ffn_up_kernel, tiles_per_seq=seq // tm, chunk=chunk, rows=rows),
        out_shape=jax.ShapeDtypeStruct((t, f), BF16),
        grid=(nj + 1, ni),
        in_specs=[pl.BlockSpec((tm, k), lambda j, i: (i, 0)),
                  _streamed_weight_spec(layer, k, tn, ni, nj, 0),
                  _streamed_weight_spec(layer, k, tn, ni, nj, nj),
                  col_spec(CONV_WIDTH, 0), col_spec(CONV_WIDTH, nj),
                  col_spec(1, 0), col_spec(1, nj)],
        out_specs=_streamed_out_spec(tm, tn),
        scratch_shapes=[pltpu.VMEM((SUBLANES, tn), F32), pltpu.VMEM((SUBLANES, tn), F32),
                        pltpu.VMEM((2, k, tn), BF16), pltpu.VMEM((2, k, tn), BF16),
                        pltpu.VMEM((2, rows, chunk), F32), pltpu.VMEM((2, rows, chunk), F32)],
        compiler_params=_params(("arbitrary", "arbitrary")),
        name="ffn_up_conv_act",
    )(h, w_up, w_up, conv_w, conv_w, cb, cb)


def conv_ffn_block(x, norm_g, w_up, w_down, layer, conv_w, conv_b, seq):
    h = rmsnorm(x, norm_g, BF16)
    act = ffn_up(h, w_up, layer, conv_w, conv_b, seq)
    return matmul_residual(act, w_down, layer, x, tm=1024, tn=256, a_buffers=1)


def hybrid_attention_block(x, positions, norm_g, w_in, w_out, layer, bsz, seq):
    h = rmsnorm(x, norm_g, BF16)
    cos, sin_lo, sin_hi = rotary_tables(positions)
    nh = N_HEADS_QKV
    qkv = qkv_projection(h, w_in, layer, cos, sin_lo, sin_hi,
                         n_q_cols=nh * HEAD_DIM, n_rot_cols=2 * nh * HEAD_DIM)
    o = moba_attention(qkv, bsz, seq, N_HEADS_A, 0, nh, 2 * nh, N_HEADS_A + N_HEADS_B_GROUP)
    o = dilated_attention(qkv, o, bsz, seq, N_HEADS_A, nh + N_HEADS_A, 2 * nh + N_HEADS_A, N_HEADS_A)
    return matmul_residual(o, w_out, layer, x, tm=1024, tn=1024)


def spatial_gating_block(x, norm_g, w_in, w_out, layer, b_in, v_gain, v_bias, w_s, b_s):
    h = rmsnorm(x, norm_g, BF16)
    z = matmul_bias_gelu(h, w_in, layer, b_in)
    gated = spatial_gating(z, v_gain, v_bias, w_s, b_s)
    return matmul_residual(gated, w_out, layer, x, tm=1024, tn=1024)


def kernel(x, positions, attn_norm, attn_w_in, attn_w_out, sg_norm, sg_w_in, sg_b_in, sg_v_gain,
           sg_v_bias, sg_w_s, sg_b_s, sg_w_out, ffn_norm, ffn_w_up, ffn_conv_w, ffn_conv_b,
           ffn_w_down, final_norm):
    bsz, seq, d = x.shape
    depth = ffn_norm.shape[0]
    attn_w_out, sg_w_out, ffn_w_down = (w.astype(BF16) for w in (attn_w_out, sg_w_out, ffn_w_down))
    h = x.reshape(bsz * seq, d)
    for layer in range(depth):
        i = layer // 2
        if layer % 2 == 0:
            h = hybrid_attention_block(h, positions, attn_norm[i], attn_w_in, attn_w_out, i, bsz, seq)
        else:
            h = spatial_gating_block(h, sg_norm[i], sg_w_in, sg_w_out, i, sg_b_in[i], sg_v_gain[i],
                                     sg_v_bias[i], sg_w_s[i], sg_b_s[i])
        h = conv_ffn_block(h, ffn_norm[layer], ffn_w_up, ffn_w_down, layer, ffn_conv_w[layer],
                           ffn_conv_b[layer], seq)
    return rmsnorm(h, final_norm, x.dtype).reshape(bsz, seq, d)
```

```python
import functools
import math

import jax
import jax.numpy as jnp
from jax import lax
from jax.experimental import pallas as pl
from jax.experimental.pallas import tpu as pltpu

F32 = jnp.float32
BF16 = jnp.bfloat16

HEAD_DIM = 128
ROT_DIM = HEAD_DIM // 4
ROPE_THETA = 500000.0
N_HEADS_A = 24
N_HEADS_B_GROUP = 8
B_PATTERNS = ((128, 1), (512, 4), (2048, 16))
N_HEADS_QKV = N_HEADS_A + N_HEADS_B_GROUP * len(B_PATTERNS)
MOBA_BLOCK = 256
MOBA_TOPK = 3
GMLP_CHUNK = 128
GMLP_GROUP_DIM = 128
CONV_WIDTH = 3
NORM_EPS = 1e-5

LANES = 128
SUBLANES = 8
MXU_COLS = 256
VMEM_LIMIT_BYTES = 56 * 1024 * 1024

MASK_VALUE = -1e30
LOG2_E = math.log2(math.e)
Q_SCALE = HEAD_DIM ** -0.5 * LOG2_E


def _params(semantics):
    return pltpu.CompilerParams(dimension_semantics=semantics, vmem_limit_bytes=VMEM_LIMIT_BYTES)


def _dot(a, b):
    return jnp.dot(a, b, preferred_element_type=F32)


def _dot_nt(a, b):
    return lax.dot_general(a, b, (((1,), (1,)), ((), ())), preferred_element_type=F32)


def _dot_tn(a, b):
    return lax.dot_general(a, b, (((0,), (0,)), ((), ())), preferred_element_type=F32)


def _head_slice(h):
    return slice(h * HEAD_DIM, (h + 1) * HEAD_DIM)


def _rmsnorm_kernel(x_ref, g_ref, o_ref):
    x = x_ref[...]
    y = x * lax.rsqrt(jnp.mean(x * x, axis=-1, keepdims=True) + NORM_EPS)
    o_ref[...] = (y * g_ref[...]).astype(o_ref.dtype)


def rmsnorm(x, g, out_dtype, rows=256):
    t, d = x.shape
    rows = min(rows, t)
    return pl.pallas_call(
        _rmsnorm_kernel,
        out_shape=jax.ShapeDtypeStruct((t, d), out_dtype),
        grid=(t // rows,),
        in_specs=[pl.BlockSpec((rows, d), lambda i: (i, 0)),
                  pl.BlockSpec((1, d), lambda i: (0, 0))],
        out_specs=pl.BlockSpec((rows, d), lambda i: (i, 0)),
        compiler_params=_params(("parallel",)),
        name="rmsnorm",
    )(x, g.reshape(1, d))


def _qkv_kernel(a_ref, w32_ref, cos_ref, sin_lo_ref, sin_hi_ref, o_ref, w_ref, *, n_q_tiles, n_rot_tiles):
    col, cur = _stream_weight_rows(w32_ref, w_ref)
    tn = o_ref.shape[1]

    @pl.when((col >= 0) & (col < n_rot_tiles))
    def _():
        sc = jnp.where(col < n_q_tiles, Q_SCALE, 1.0).astype(F32)
        cos = cos_ref[...] * sc
        sin_lo = sin_lo_ref[...] * sc
        sin_hi = sin_hi_ref[...] * sc
        for c0 in range(0, tn, MXU_COLS):
            x2 = _dot(a_ref[...], w_ref[cur, :, c0:c0 + MXU_COLS])
            for h in range(MXU_COLS // HEAD_DIM):
                x = x2[:, _head_slice(h)]
                up = pltpu.roll(x, HEAD_DIM - ROT_DIM // 2, axis=1)
                down = pltpu.roll(x, ROT_DIM // 2, axis=1)
                lo = c0 + h * HEAD_DIM
                o_ref[:, lo:lo + HEAD_DIM] = (x * cos + up * sin_lo + down * sin_hi).astype(o_ref.dtype)

    @pl.when(col >= n_rot_tiles)
    def _():
        o_ref[...] = _dot(a_ref[...], w_ref[cur]).astype(o_ref.dtype)


def qkv_projection(h, w, layer, cos, sin_lo, sin_hi, n_q_cols, n_rot_cols, tm=1024, tn=1024):
    t, k = h.shape
    n = w.shape[2]
    tm, tn = min(tm, t), min(tn, n)
    ni, nj = t // tm, n // tn
    assert n_q_cols % tn == 0 and n_rot_cols % tn == 0 and tn % MXU_COLS == 0
    return pl.pallas_call(
        functools.partial(_qkv_kernel, n_q_tiles=n_q_cols // tn, n_rot_tiles=n_rot_cols // tn),
        out_shape=jax.ShapeDtypeStruct((t, n), BF16),
        grid=(nj + 1, ni),
        in_specs=[pl.BlockSpec((tm, k), lambda j, i: (i, 0)),
                  _streamed_weight_spec(layer, k, tn, ni, nj, 0),
                  pl.BlockSpec((tm, HEAD_DIM), lambda j, i: (i, 0)),
                  pl.BlockSpec((tm, HEAD_DIM), lambda j, i: (i, 0)),
                  pl.BlockSpec((tm, HEAD_DIM), lambda j, i: (i, 0))],
        out_specs=_streamed_out_spec(tm, tn),
        scratch_shapes=[pltpu.VMEM((2, k, tn), BF16)],
        compiler_params=_params(("arbitrary", "arbitrary")),
        name="qkv_rotary",
    )(h, w, cos, sin_lo, sin_hi)


def rotary_tables(positions):
    half = ROT_DIM // 2
    inv_freq = jnp.power(ROPE_THETA, -jnp.arange(half, dtype=F32) * (2.0 / ROT_DIM))
    ang = positions.reshape(-1).astype(F32)[:, None] * inv_freq
    cos, sin = jnp.cos(ang), jnp.sin(ang)
    t = ang.shape[0]
    ones = jnp.ones((t, HEAD_DIM - ROT_DIM), F32)
    zeros = jnp.zeros((t, HEAD_DIM - ROT_DIM), F32)
    zh = jnp.zeros((t, half), F32)
    cos_t = jnp.concatenate([cos, cos, ones], axis=1)
    sin_lo = jnp.concatenate([-sin, zh, zeros], axis=1)
    sin_hi = jnp.concatenate([zh, sin, zeros], axis=1)
    return cos_t, sin_lo, sin_hi


def _mm_resid_kernel(a_ref, w_ref, r_ref, o_ref):
    o_ref[...] = r_ref[...] + _dot(a_ref[...], w_ref[...])


def matmul_residual(a, w, layer, resid, tm, tn, a_buffers=2):
    t, k = a.shape
    n = w.shape[2]
    tm, tn = min(tm, t), min(tn, n)
    return pl.pallas_call(
        _mm_resid_kernel,
        out_shape=jax.ShapeDtypeStruct((t, n), F32),
        grid=(t // tm, n // tn),
        in_specs=[pl.BlockSpec((tm, k), lambda i, j: (i, 0), pipeline_mode=pl.Buffered(a_buffers)),
                  pl.BlockSpec((None, k, tn), lambda i, j: (layer, 0, j)),
                  pl.BlockSpec((tm, tn), lambda i, j: (i, j))],
        out_specs=pl.BlockSpec((tm, tn), lambda i, j: (i, j)),
        compiler_params=_params(("parallel", "arbitrary")),
        name="matmul_residual",
    )(a, w, resid)


def _mm_bias_gelu_kernel(a_ref, w32_ref, b_ref, o_ref, w_ref):
    col, cur = _stream_weight_rows(w32_ref, w_ref)

    @pl.when(col >= 0)
    def _():
        z = _dot(a_ref[...], w_ref[cur]) + b_ref[...]
        o_ref[...] = (0.5 * z * (1.0 + lax.erf(z * (2.0 ** -0.5)))).astype(o_ref.dtype)


def matmul_bias_gelu(a, w, layer, b, tm=1024, tn=1024):
    t, k = a.shape
    n = w.shape[2]
    tm, tn = min(tm, t), min(tn, n)
    ni, nj = t // tm, n // tn
    return pl.pallas_call(
        _mm_bias_gelu_kernel,
        out_shape=jax.ShapeDtypeStruct((t, n), BF16),
        grid=(nj + 1, ni),
        in_specs=[pl.BlockSpec((tm, k), lambda j, i: (i, 0)),
                  _streamed_weight_spec(layer, k, tn, ni, nj, 0),
                  pl.BlockSpec((1, tn), lambda j, i: (0, jnp.maximum(j - 1, 0)))],
        out_specs=_streamed_out_spec(tm, tn),
        scratch_shapes=[pltpu.VMEM((2, k, tn), BF16)],
        compiler_params=_params(("arbitrary", "arbitrary")),
        name="matmul_bias_gelu",
    )(a, w, b.reshape(1, n))


def _ones_augmented(v):
    return jnp.concatenate([v, jnp.ones_like(v)], axis=1)


def _softmax_init(s, vaug):
    m = jnp.max(s, axis=1, keepdims=True)
    p = jnp.exp2(s - m)
    return m, _dot(p.astype(vaug.dtype), vaug)


def _softmax_update(state, s, vaug):
    m, accl = state
    m_new = jnp.maximum(m, jnp.max(s, axis=1, keepdims=True))
    alpha = jnp.exp2(m - m_new)
    p = jnp.exp2(s - m_new)
    return m_new, alpha * accl + _dot(p.astype(vaug.dtype), vaug)


def _softmax_finish(state):
    _, accl = state
    return accl[:, :HEAD_DIM] / accl[:, HEAD_DIM:]


MOBA_HEADS_PER_STEP = 4
MOBA_BLOCKS_PER_ITER = 2


MOBA_VT_ROWS = HEAD_DIM + 16


def _column_max(x):
    rows, cols = x.shape
    x = x.reshape(rows // SUBLANES, SUBLANES, cols)
    while x.shape[0] > 1:
        half = x.shape[0] // 2
        x = jnp.maximum(x[:half], x[half:])
    return jnp.max(x[0], axis=0, keepdims=True)


def _moba_kernel(q_ref, k_ref, v_ref, o_ref, kmean_ref, vt_ref, *, n_blocks, heads):
    i = pl.program_id(2)
    blk = MOBA_BLOCK
    per_iter = MOBA_BLOCKS_PER_ITER

    @pl.when(i == 0)
    def _():
        for h in range(heads):
            hs = _head_slice(h)
            kf = k_ref[:, hs].astype(F32).reshape(n_blocks, blk, HEAD_DIM)
            kmean_ref[h] = jnp.mean(kf, axis=1)
            for n in range(n_blocks):
                vt = v_ref[n * blk:(n + 1) * blk, hs].astype(F32).T
                vt_ref[h, n, 0:HEAD_DIM, :] = vt.astype(BF16)
                vt_ref[h, n, HEAD_DIM:MOBA_VT_ROWS, :] = jnp.ones((MOBA_VT_ROWS - HEAD_DIM, blk), BF16)

    key_row = lax.broadcasted_iota(jnp.int32, (blk, blk), 0)
    query_col = lax.broadcasted_iota(jnp.int32, (blk, blk), 1)
    causal = key_row <= query_col
    blk_idx = lax.broadcasted_iota(jnp.int32, (n_blocks, blk), 0)
    own = pl.ds(pl.multiple_of(i * blk, blk), blk)

    def gate_scores(q, h):
        km = kmean_ref[h]
        km_hi = km.astype(BF16)
        km_lo = (km - km_hi.astype(F32)).astype(BF16)
        return _dot_nt(km_hi, q) + _dot_nt(km_lo, q)

    def select_bits(gate):
        rank = jnp.zeros(gate.shape, jnp.int32)
        for m in range(n_blocks):
            gm = gate[m:m + 1, :]
            beats = (gm > gate) | ((gm == gate) & (m < blk_idx))
            rank = rank + jnp.where(beats & (m < i), 1, 0)
        selected = (blk_idx < i) & (rank < MOBA_TOPK)
        return jnp.sum(jnp.where(selected, jnp.left_shift(1, blk_idx), 0), axis=0, keepdims=True)

    qs = [q_ref[:, _head_slice(h)] for h in range(heads)]
    gates = [gate_scores(qs[h], h) for h in range(heads)]
    own_scores = [_dot_nt(k_ref[own, _head_slice(h)], qs[h]) for h in range(heads)]
    bits = [select_bits(gates[h]) for h in range(heads)]

    states = []
    for h in range(heads):
        st = jnp.where(causal, own_scores[h], MASK_VALUE)
        m = _column_max(st)
        p = jnp.exp2(st - m).astype(BF16)
        states.append((m, _dot(vt_ref[h, i], p)))

    def body(c, states):
        rows = pl.ds(pl.multiple_of(c * (per_iter * blk), per_iter * blk), per_iter * blk)
        new_states = []
        scores = [_dot_nt(k_ref[rows, _head_slice(h)], qs[h]) for h in range(heads)]
        for h in range(heads):
            m, acc = states[h]
            st = scores[h]
            pieces = []
            for t in range(per_iter):
                keep = jnp.left_shift(bits[h], 31 - (c * per_iter + t)) < 0
                pieces.append(jnp.where(keep, st[t * blk:(t + 1) * blk], MASK_VALUE))
            st = jnp.concatenate(pieces, axis=0)
            m_new = jnp.maximum(m, _column_max(st))
            p = jnp.exp2(st - m_new).astype(BF16)
            acc = jnp.exp2(m - m_new) * acc
            for t in range(per_iter):
                acc = acc + _dot(vt_ref[h, c * per_iter + t], p[t * blk:(t + 1) * blk])
            new_states.append((m_new, acc))
        return tuple(new_states)

    n_iters = (i + per_iter - 1) // per_iter
    states = lax.fori_loop(0, n_iters, body, tuple(states))
    for h in range(heads):
        _, acc = states[h]
        out_t = acc[:HEAD_DIM] / acc[HEAD_DIM:HEAD_DIM + 1]
        o_ref[:, _head_slice(h)] = out_t.T.astype(o_ref.dtype)


def moba_attention(qkv, bsz, seq, n_heads, q_head0, k_head0, v_head0, n_out_heads):
    nb = seq // MOBA_BLOCK
    hp = MOBA_HEADS_PER_STEP
    assert n_heads % hp == 0 and q_head0 % hp == 0 and k_head0 % hp == 0 and v_head0 % hp == 0
    assert nb % MOBA_BLOCKS_PER_ITER == 0
    w = hp * HEAD_DIM
    return pl.pallas_call(
        functools.partial(_moba_kernel, n_blocks=nb, heads=hp),
        out_shape=jax.ShapeDtypeStruct((bsz * seq, n_out_heads * HEAD_DIM), BF16),
        grid=(bsz, n_heads // hp, nb),
        in_specs=[pl.BlockSpec((MOBA_BLOCK, w), lambda b, h, i: (b * nb + i, q_head0 // hp + h)),
                  pl.BlockSpec((seq, w), lambda b, h, i: (b, k_head0 // hp + h)),
                  pl.BlockSpec((seq, w), lambda b, h, i: (b, v_head0 // hp + h))],
        out_specs=pl.BlockSpec((MOBA_BLOCK, w), lambda b, h, i: (b * nb + i, h)),
        scratch_shapes=[pltpu.VMEM((hp, nb, HEAD_DIM), F32),
                        pltpu.VMEM((hp, nb, MOBA_VT_ROWS, MOBA_BLOCK), BF16)],
        compiler_params=_params(("parallel", "parallel", "arbitrary")),
        name="moba_attention",
    )(qkv, qkv, qkv)


DIL_TILE = 256
DIL_BACK = 128


def _residue_permutation(dil):
    per = DIL_TILE // dil
    r = lax.broadcasted_iota(jnp.int32, (DIL_TILE, DIL_TILE), 0)
    c = lax.broadcasted_iota(jnp.int32, (DIL_TILE, DIL_TILE), 1)
    return (c == (r % per) * dil + r // per).astype(BF16)


DIL_TILES_PER_ITER = 4


def _band_tiles(tiles):
    scores = [_dot_nt(q, k) for q, k, _, _, _ in tiles]
    results = []
    for s, (_, k, vaug, q0, k0) in zip(scores, tiles):
        n = k.shape[0]
        dist = (q0 - k0) + (lax.broadcasted_iota(jnp.int32, (DIL_TILE, n), 0)
                            - lax.broadcasted_iota(jnp.int32, (DIL_TILE, n), 1))
        s = jnp.where((dist >= 0) & (dist <= DIL_BACK), s, MASK_VALUE)
        m = jnp.max(s, axis=1, keepdims=True)
        p = jnp.exp2(s - m)
        ol = _dot(p.astype(vaug.dtype), vaug)
        l = ol[:, HEAD_DIM:]
        results.append((ol[:, :HEAD_DIM] / l, m + jnp.log(l) * LOG2_E))
    return results


def _dilated_kernel(q0_ref, q1_ref, q2_ref, k0_ref, k1_ref, k2_ref, v0_ref, v1_ref, v2_ref, _buf_ref,
                    o_ref, v0aug_ref, qs1_ref, ks1_ref, vs1_ref, qs2_ref, ks2_ref, vs2_ref, out_ref, lse_ref, *, seq):
    tile = DIL_TILE
    n_tiles = seq // tile
    keys = tile + DIL_BACK
    groups = ((q1_ref, k1_ref, v1_ref, qs1_ref, ks1_ref, vs1_ref),
              (q2_ref, k2_ref, v2_ref, qs2_ref, ks2_ref, vs2_ref))

    v0aug_ref[...] = _ones_augmented(v0_ref[...])
    for (_, dil), refs in zip(B_PATTERNS[1:], groups):
        per = tile // dil
        perm = _residue_permutation(dil)

        def gather(b, carry, per=per, perm=perm, refs=refs, dil=dil):
            q_ref, k_ref, v_ref, qs_ref, ks_ref, vs_ref = refs
            rows = pl.ds(pl.multiple_of(b * tile, tile), tile)
            dst = pl.ds(pl.multiple_of(b * per, per), per)
            qkv = jnp.concatenate([q_ref[rows, :], k_ref[rows, :], v_ref[rows, :]], axis=1)
            moved = _dot(perm, qkv).astype(BF16)
            qs_ref[:, dst, :] = moved[:, _head_slice(0)].reshape(dil, per, HEAD_DIM)
            ks_ref[:, dst, :] = moved[:, _head_slice(1)].reshape(dil, per, HEAD_DIM)
            vs_ref[:, dst, :] = _ones_augmented(moved[:, _head_slice(2)]).reshape(dil, per, 2 * HEAD_DIM)
            return carry

        lax.fori_loop(0, n_tiles, gather, 0)

    def window(t0):
        k0 = jnp.maximum(t0 - DIL_BACK, 0)
        return k0, pl.ds(pl.multiple_of(k0, DIL_BACK), keys)

    per_iter = DIL_TILES_PER_ITER

    def tiles_g0(it, carry):
        tiles, dsts = [], []
        for u in range(per_iter):
            t0 = pl.multiple_of((it * per_iter + u) * tile, tile)
            k0, krows = window(t0)
            tiles.append((q0_ref[pl.ds(t0, tile), :], k0_ref[krows, :], v0aug_ref[krows, :], t0, k0))
            dsts.append(pl.ds(t0, tile))
        for dst, (out, lse) in zip(dsts, _band_tiles(tiles)):
            out_ref[0, dst, :] = out
            lse_ref[0, dst, :] = lse
        return carry

    lax.fori_loop(0, n_tiles // per_iter, tiles_g0, 0)

    for g, ((_, dil), refs) in enumerate(zip(B_PATTERNS[1:], groups), start=1):
        sub_tiles = n_tiles // dil

        def tiles_g(it, carry, g=g, dil=dil, sub_tiles=sub_tiles, refs=refs[3:]):
            qs_ref, ks_ref, vs_ref = refs
            tiles, dsts = [], []
            for u in range(per_iter):
                n = it * per_iter + u
                res, ti = n // sub_tiles, n % sub_tiles
                t0 = pl.multiple_of(ti * tile, tile)
                if sub_tiles == 1:
                    k0, krows = 0, pl.ds(0, tile)
                else:
                    k0, krows = window(t0)
                tiles.append((qs_ref[res, pl.ds(t0, tile), :], ks_ref[res, krows, :], vs_ref[res, krows, :],
                              t0, k0))
                dsts.append(pl.ds(t0 * dil + res, tile, stride=dil))
            for dst, (out, lse) in zip(dsts, _band_tiles(tiles)):
                out_ref[g, dst, :] = out
                lse_ref[g, dst, :] = lse
            return carry

        lax.fori_loop(0, n_tiles // per_iter, tiles_g, 0)

    def merge(i, carry):
        rows = pl.ds(pl.multiple_of(i * tile, tile), tile)
        lses = [lse_ref[g, rows, :] for g in range(len(B_PATTERNS))]
        top = functools.reduce(jnp.maximum, lses)
        ws = [jnp.exp2(lse - top) for lse in lses]
        num = sum(w * out_ref[g, rows, :] for g, w in enumerate(ws))
        o_ref[rows, :] = (num / sum(ws)).astype(o_ref.dtype)
        return carry

    lax.fori_loop(0, n_tiles, merge, 0)


def dilated_attention(qkv, out_buf, bsz, seq, q_head0, k_head0, v_head0, out_head0):
    hg = N_HEADS_B_GROUP
    ng = len(B_PATTERNS)
    tile = DIL_TILE
    assert all(w // d == DIL_BACK and tile % d == 0 and (seq // tile) % d == 0 for w, d in B_PATTERNS)
    assert B_PATTERNS[0][1] == 1 and seq % tile == 0 and seq >= tile + DIL_BACK

    def spec(head0, g):
        return pl.BlockSpec((seq, HEAD_DIM), lambda b, h, g=g: (b, head0 + g * hg + h))

    scratch = [pltpu.VMEM((seq, 2 * HEAD_DIM), BF16)]
    for _, dil in B_PATTERNS[1:]:
        scratch += [pltpu.VMEM((dil, seq // dil, HEAD_DIM), BF16), pltpu.VMEM((dil, seq // dil, HEAD_DIM), BF16),
                    pltpu.VMEM((dil, seq // dil, 2 * HEAD_DIM), BF16)]
    scratch += [pltpu.VMEM((ng, seq, HEAD_DIM), F32), pltpu.VMEM((ng, seq, HEAD_DIM), F32)]
    return pl.pallas_call(
        functools.partial(_dilated_kernel, seq=seq),
        out_shape=jax.ShapeDtypeStruct(out_buf.shape, out_buf.dtype),
        grid=(bsz, hg),
        in_specs=([spec(q_head0, g) for g in range(ng)] + [spec(k_head0, g) for g in range(ng)]
                  + [spec(v_head0, g) for g in range(ng)] + [pl.BlockSpec(memory_space=pl.ANY)]),
        out_specs=pl.BlockSpec((seq, HEAD_DIM), lambda b, h: (b, out_head0 + h)),
        scratch_shapes=scratch,
        input_output_aliases={3 * ng: 0},
        compiler_params=_params(("parallel", "parallel")),
        name="dilated_attention",
    )(*([qkv] * (3 * ng)), out_buf)


def _sgu_kernel(u_ref, v_ref, gain_ref, bias_ref, ws_ref, bs_ref, o_ref):
    v = v_ref[...].astype(F32)
    mu = jnp.mean(v, axis=-1, keepdims=True)
    vc = v - mu
    vn = vc * lax.rsqrt(jnp.mean(vc * vc, axis=-1, keepdims=True) + NORM_EPS)
    vn = (vn * gain_ref[...] + bias_ref[...]).astype(BF16)
    c = GMLP_CHUNK
    row = lax.broadcasted_iota(jnp.int32, (c, c), 0)
    col = lax.broadcasted_iota(jnp.int32, (c, c), 1)
    causal = col <= row
    for g in range(ws_ref.shape[0]):
        sl = slice(g * GMLP_GROUP_DIM, (g + 1) * GMLP_GROUP_DIM)
        w = jnp.where(causal, ws_ref[g], 0.0).astype(BF16)
        f = _dot(w, vn[:, sl]) + bs_ref[:, g:g + 1]
        o_ref[:, sl] = (u_ref[:, sl].astype(F32) * f).astype(o_ref.dtype)


def spatial_gating(z, gain, bias, w_s, b_s):
    t, e2 = z.shape
    e = e2 // 2
    groups = w_s.shape[0]
    c = GMLP_CHUNK
    return pl.pallas_call(
        _sgu_kernel,
        out_shape=jax.ShapeDtypeStruct((t, e), BF16),
        grid=(t // c,),
        in_specs=[pl.BlockSpec((c, e), lambda i: (i, 0)),
                  pl.BlockSpec((c, e), lambda i: (i, 1)),
                  pl.BlockSpec((1, e), lambda i: (0, 0)),
                  pl.BlockSpec((1, e), lambda i: (0, 0)),
                  pl.BlockSpec((groups, c, c), lambda i: (0, 0, 0)),
                  pl.BlockSpec((c, groups), lambda i: (0, 0))],
        out_specs=pl.BlockSpec((c, e), lambda i: (i, 0)),
        compiler_params=_params(("parallel",)),
        name="spatial_gating",
    )(z, z, gain.reshape(1, e), bias.reshape(1, e), w_s, b_s.T)


def _causal_conv(a, prev, cw, cb):
    tm, n = a.shape
    a3 = a.reshape(tm // SUBLANES, SUBLANES, n)
    row_in_group = lax.broadcasted_iota(jnp.int32, (1, SUBLANES, n), 1)
    out = cb + a * cw[CONV_WIDTH - 1:CONV_WIDTH]
    for back in range(1, CONV_WIDTH):
        rot = pltpu.roll(a3, back, axis=1)
        before = jnp.concatenate([pltpu.roll(prev, back, axis=0)[None], rot[:-1]], axis=0)
        shifted = jnp.where(row_in_group < back, before, rot).reshape(tm, n)
        out = out + shifted * cw[CONV_WIDTH - 1 - back:CONV_WIDTH - back]
    return out


def _streamed_weight_spec(layer, k, tn, ni, nj, col0):
    assert k % ni == 0
    return pl.BlockSpec((None, k // ni, tn), lambda j, i: (layer, i, col0 + jnp.minimum(j, nj - 1)))


FFN_SUBTILE_ROWS = 256


def _streamed_out_spec(tm, tn):
    return pl.BlockSpec((tm, tn), lambda j, i: (jnp.where(j == 0, 0, i), jnp.maximum(j - 1, 0)))


def _stream_weight_rows(*refs):
    j, i = pl.program_id(0), pl.program_id(1)
    fill = j % 2
    for w32_ref, w_ref in zip(refs[0::2], refs[1::2]):
        chunk = w32_ref.shape[0]
        w_ref[fill, pl.ds(pl.multiple_of(i * chunk, chunk), chunk), :] = w32_ref[...].astype(w_ref.dtype)
    return j - 1, 1 - fill


def _ffn_up_kernel(h_ref, wg32_ref, wu32_ref, cwg_ref, cwu_ref, cbg_ref, cbu_ref, o_ref, pg_ref, pu_ref,
                   wg_ref, wu_ref, ag_ref, au_ref, *, tiles_per_seq, chunk, rows):
    i = pl.program_id(1)
    tm, tn = o_ref.shape
    col, cur = _stream_weight_rows(wg32_ref, wg_ref, wu32_ref, wu_ref)

    @pl.when(i % tiles_per_seq == 0)
    def _():
        pg_ref[...] = jnp.zeros(pg_ref.shape, F32)
        pu_ref[...] = jnp.zeros(pu_ref.shape, F32)

    @pl.when(col >= 0)
    def _():
        tiles = [(c0, r0) for c0 in range(0, tn, chunk) for r0 in range(0, tm, rows)]

        def matmuls(n):
            c0, r0 = tiles[n]
            sl, rs = slice(c0, c0 + chunk), slice(r0, r0 + rows)
            ag_ref[(n + cur) % 2] = _dot(h_ref[rs, :], wg_ref[cur, :, sl])
            au_ref[(n + cur) % 2] = _dot(h_ref[rs, :], wu_ref[cur, :, sl])

        matmuls(0)
        prev_g = prev_u = None
        for n, (c0, r0) in enumerate(tiles):
            sl, rs = slice(c0, c0 + chunk), slice(r0, r0 + rows)
            if n + 1 < len(tiles):
                matmuls(n + 1)
            if r0 == 0:
                prev_g, prev_u = pg_ref[:, sl], pu_ref[:, sl]
            ag, au = ag_ref[(n + cur) % 2], au_ref[(n + cur) % 2]
            gate = _causal_conv(ag, prev_g, cwg_ref[:, sl], cbg_ref[:, sl])
            up = _causal_conv(au, prev_u, cwu_ref[:, sl], cbu_ref[:, sl])
            o_ref[rs, sl] = (jax.nn.silu(gate) * up).astype(o_ref.dtype)
            prev_g, prev_u = ag[rows - SUBLANES:rows], au[rows - SUBLANES:rows]
            if r0 + rows == tm:
                pg_ref[:, sl] = prev_g
                pu_ref[:, sl] = prev_u


def ffn_up(h, w_up, layer, conv_w, conv_b, seq, tm=1024, tn=512, chunk=MXU_COLS, rows=FFN_SUBTILE_ROWS):
    t, k = h.shape
    f = w_up.shape[2] // 2
    tm, tn = min(tm, seq), min(tn, f)
    chunk, rows = min(chunk, tn), min(rows, tm)
    ni, nj = t // tm, f // tn
    assert seq % tm == 0 and f % tn == 0 and tn % chunk == 0 and tm % rows == 0
    cb = conv_b.reshape(1, 2 * f)

    def col_spec(rows_, col0):
        return pl.BlockSpec((rows_, tn), lambda j, i: (0, col0 + jnp.maximum(j - 1, 0)))

    return pl.pallas_call(
        functools.partial(_ffn_up_kernel, tiles_per_seq=seq // tm, chunk=chunk, rows=rows),
        out_shape=jax.ShapeDtypeStruct((t, f), BF16),
        grid=(nj + 1, ni),
        in_specs=[pl.BlockSpec((tm, k), lambda j, i: (i, 0)),
                  _streamed_weight_spec(layer, k, tn, ni, nj, 0),
                  _streamed_weight_spec(layer, k, tn, ni, nj, nj),
                  col_spec(CONV_WIDTH, 0), col_spec(CONV_WIDTH, nj),
                  col_spec(1, 0), col_spec(1, nj)],
        out_specs=_streamed_out_spec(tm, tn),
        scratch_shapes=[pltpu.VMEM((SUBLANES, tn), F32), pltpu.VMEM((SUBLANES, tn), F32),
                        pltpu.VMEM((2, k, tn), BF16), pltpu.VMEM((2, k, tn), BF16),
                        pltpu.VMEM((2, rows, chunk), F32), pltpu.VMEM((2, rows, chunk), F32)],
        compiler_params=_params(("arbitrary", "arbitrary")),
        name="ffn_up_conv_act",
    )(h, w_up, w_up, conv_w, conv_w, cb, cb)


def conv_ffn_block(x, norm_g, w_up, w_down, layer, conv_w, conv_b, seq):
    h = rmsnorm(x, norm_g, BF16)
    act = ffn_up(h, w_up, layer, conv_w, conv_b, seq)
    return matmul_residual(act, w_down, layer, x, tm=1024, tn=256, a_buffers=1)


def hybrid_attention_block(x, positions, norm_g, w_in, w_out, layer, bsz, seq):
    h = rmsnorm(x, norm_g, BF16)
    cos, sin_lo, sin_hi = rotary_tables(positions)
    nh = N_HEADS_QKV
    qkv = qkv_projection(h, w_in, layer, cos, sin_lo, sin_hi,
                         n_q_cols=nh * HEAD_DIM, n_rot_cols=2 * nh * HEAD_DIM)
    o = moba_attention(qkv, bsz, seq, N_HEADS_A, 0, nh, 2 * nh, N_HEADS_A + N_HEADS_B_GROUP)
    o = dilated_attention(qkv, o, bsz, seq, N_HEADS_A, nh + N_HEADS_A, 2 * nh + N_HEADS_A, N_HEADS_A)
    return matmul_residual(o, w_out, layer, x, tm=1024, tn=1024)


def spatial_gating_block(x, norm_g, w_in, w_out, layer, b_in, v_gain, v_bias, w_s, b_s):
    h = rmsnorm(x, norm_g, BF16)
    z = matmul_bias_gelu(h, w_in, layer, b_in)
    gated = spatial_gating(z, v_gain, v_bias, w_s, b_s)
    return matmul_residual(gated, w_out, layer, x, tm=1024, tn=1024)


def kernel(x, positions, attn_norm, attn_w_in, attn_w_out, sg_norm, sg_w_in, sg_b_in, sg_v_gain,
           sg_v_bias, sg_w_s, sg_b_s, sg_w_out, ffn_norm, ffn_w_up, ffn_conv_w, ffn_conv_b,
           ffn_w_down, final_norm):
    bsz, seq, d = x.shape
    depth = ffn_norm.shape[0]
    attn_w_out, sg_w_out, ffn_w_down = (w.astype(BF16) for w in (attn_w_out, sg_w_out, ffn_w_down))
    h = x.reshape(bsz * seq, d)
    for layer in range(depth):
        i = layer // 2
        if layer % 2 == 0:
            h = hybrid_attention_block(h, positions, attn_norm[i], attn_w_in, attn_w_out, i, bsz, seq)
        else:
            h = spatial_gating_block(h, sg_norm[i], sg_w_in, sg_w_out, i, sg_b_in[i], sg_v_gain[i],
                                     sg_v_bias[i], sg_w_s[i], sg_b_s[i])
        h = conv_ffn_block(h, ffn_norm[layer], ffn_w_up, ffn_w_down, layer, ffn_conv_w[layer],
                           ffn_conv_b[layer], seq)
    return rmsnorm(h, final_norm, x.dtype).reshape(bsz, seq, d)
```

```python
import functools
import math

import jax
import jax.numpy as jnp
from jax import lax
from jax.experimental import pallas as pl
from jax.experimental.pallas import tpu as pltpu

F32 = jnp.float32
BF16 = jnp.bfloat16

HEAD_DIM = 128
ROT_DIM = HEAD_DIM // 4
ROPE_THETA = 500000.0
N_HEADS_A = 24
N_HEADS_B_GROUP = 8
B_PATTERNS = ((128, 1), (512, 4), (2048, 16))
N_HEADS_QKV = N_HEADS_A + N_HEADS_B_GROUP * len(B_PATTERNS)
MOBA_BLOCK = 256
MOBA_TOPK = 3
GMLP_CHUNK = 128
GMLP_GROUP_DIM = 128
CONV_WIDTH = 3
NORM_EPS = 1e-5

SUBLANES = 8
MXU_COLS = 256
VMEM_LIMIT_BYTES = 56 * 1024 * 1024

MASK_VALUE = -1e30
LOG2_E = math.log2(math.e)
Q_SCALE = HEAD_DIM ** -0.5 * LOG2_E


def _params(semantics):
    return pltpu.CompilerParams(dimension_semantics=semantics, vmem_limit_bytes=VMEM_LIMIT_BYTES)


def _dot(a, b):
    return jnp.dot(a, b, preferred_element_type=F32)


def _dot_nt(a, b):
    return lax.dot_general(a, b, (((1,), (1,)), ((), ())), preferred_element_type=F32)


def _head_slice(h):
    return slice(h * HEAD_DIM, (h + 1) * HEAD_DIM)


def _rmsnorm_kernel(x_ref, g_ref, o_ref):
    x = x_ref[...]
    y = x * lax.rsqrt(jnp.mean(x * x, axis=-1, keepdims=True) + NORM_EPS)
    o_ref[...] = (y * g_ref[...]).astype(o_ref.dtype)


def rmsnorm(x, g, out_dtype, rows=512):
    t, d = x.shape
    rows = min(rows, t)
    return pl.pallas_call(
        _rmsnorm_kernel,
        out_shape=jax.ShapeDtypeStruct((t, d), out_dtype),
        grid=(t // rows,),
        in_specs=[pl.BlockSpec((rows, d), lambda i: (i, 0)),
                  pl.BlockSpec((1, d), lambda i: (0, 0))],
        out_specs=pl.BlockSpec((rows, d), lambda i: (i, 0)),
        compiler_params=_params(("parallel",)),
        name="rmsnorm",
    )(x, g.reshape(1, d))


def _qkv_kernel(a_ref, w32_ref, cos_ref, sin_lo_ref, sin_hi_ref, o_ref, w_ref, *, n_q_tiles, n_rot_tiles):
    col, cur = _stream_weight_rows(w32_ref, w_ref)
    tn = o_ref.shape[1]

    @pl.when((col >= 0) & (col < n_rot_tiles))
    def _():
        sc = jnp.where(col < n_q_tiles, Q_SCALE, 1.0).astype(F32)
        cos = cos_ref[...] * sc
        sin_lo = sin_lo_ref[...] * sc
        sin_hi = sin_hi_ref[...] * sc
        for c0 in range(0, tn, MXU_COLS):
            x2 = _dot(a_ref[...], w_ref[cur, :, c0:c0 + MXU_COLS])
            for h in range(MXU_COLS // HEAD_DIM):
                x = x2[:, _head_slice(h)]
                up = pltpu.roll(x, HEAD_DIM - ROT_DIM // 2, axis=1)
                down = pltpu.roll(x, ROT_DIM // 2, axis=1)
                lo = c0 + h * HEAD_DIM
                o_ref[:, lo:lo + HEAD_DIM] = (x * cos + up * sin_lo + down * sin_hi).astype(o_ref.dtype)

    @pl.when(col >= n_rot_tiles)
    def _():
        o_ref[...] = _dot(a_ref[...], w_ref[cur]).astype(o_ref.dtype)


def qkv_projection(h, w, layer, cos, sin_lo, sin_hi, n_q_cols, n_rot_cols, tm=1024, tn=1024):
    t, k = h.shape
    n = w.shape[2]
    tm, tn = min(tm, t), min(tn, n)
    ni, nj = t // tm, n // tn
    assert n_q_cols % tn == 0 and n_rot_cols % tn == 0 and tn % MXU_COLS == 0
    return pl.pallas_call(
        functools.partial(_qkv_kernel, n_q_tiles=n_q_cols // tn, n_rot_tiles=n_rot_cols // tn),
        out_shape=jax.ShapeDtypeStruct((t, n), BF16),
        grid=(nj + 1, ni),
        in_specs=[pl.BlockSpec((tm, k), lambda j, i: (i, 0)),
                  _streamed_weight_spec(layer, k, tn, ni, nj, 0),
                  pl.BlockSpec((tm, HEAD_DIM), lambda j, i: (i, 0)),
                  pl.BlockSpec((tm, HEAD_DIM), lambda j, i: (i, 0)),
                  pl.BlockSpec((tm, HEAD_DIM), lambda j, i: (i, 0))],
        out_specs=_streamed_out_spec(tm, tn),
        scratch_shapes=[pltpu.VMEM((2, k, tn), BF16)],
        compiler_params=_params(("arbitrary", "arbitrary")),
        name="qkv_rotary",
    )(h, w, cos, sin_lo, sin_hi)


def rotary_tables(positions):
    half = ROT_DIM // 2
    inv_freq = jnp.power(ROPE_THETA, -jnp.arange(half, dtype=F32) * (2.0 / ROT_DIM))
    ang = positions.reshape(-1).astype(F32)[:, None] * inv_freq
    cos, sin = jnp.cos(ang), jnp.sin(ang)
    t = ang.shape[0]
    ones = jnp.ones((t, HEAD_DIM - ROT_DIM), F32)
    zeros = jnp.zeros((t, HEAD_DIM - ROT_DIM), F32)
    zh = jnp.zeros((t, half), F32)
    cos_t = jnp.concatenate([cos, cos, ones], axis=1)
    sin_lo = jnp.concatenate([-sin, zh, zeros], axis=1)
    sin_hi = jnp.concatenate([zh, sin, zeros], axis=1)
    return cos_t, sin_lo, sin_hi


def _mm_resid_kernel(a_ref, w_ref, r_ref, o_ref):
    o_ref[...] = r_ref[...] + _dot(a_ref[...], w_ref[...])


def matmul_residual(a, w, layer, resid, tm, tn, a_buffers=2):
    t, k = a.shape
    n = w.shape[2]
    tm, tn = min(tm, t), min(tn, n)
    return pl.pallas_call(
        _mm_resid_kernel,
        out_shape=jax.ShapeDtypeStruct((t, n), F32),
        grid=(t // tm, n // tn),
        in_specs=[pl.BlockSpec((tm, k), lambda i, j: (i, 0), pipeline_mode=pl.Buffered(a_buffers)),
                  pl.BlockSpec((None, k, tn), lambda i, j: (layer, 0, j)),
                  pl.BlockSpec((tm, tn), lambda i, j: (i, j))],
        out_specs=pl.BlockSpec((tm, tn), lambda i, j: (i, j)),
        compiler_params=_params(("parallel", "arbitrary")),
        name="matmul_residual",
    )(a, w, resid)


def _mm_bias_gelu_kernel(a_ref, w32_ref, b_ref, o_ref, w_ref):
    col, cur = _stream_weight_rows(w32_ref, w_ref)

    @pl.when(col >= 0)
    def _():
        z = _dot(a_ref[...], w_ref[cur]) + b_ref[...]
        o_ref[...] = (0.5 * z * (1.0 + lax.erf(z * (2.0 ** -0.5)))).astype(o_ref.dtype)


def matmul_bias_gelu(a, w, layer, b, tm=1024, tn=1024):
    t, k = a.shape
    n = w.shape[2]
    tm, tn = min(tm, t), min(tn, n)
    ni, nj = t // tm, n // tn
    return pl.pallas_call(
        _mm_bias_gelu_kernel,
        out_shape=jax.ShapeDtypeStruct((t, n), BF16),
        grid=(nj + 1, ni),
        in_specs=[pl.BlockSpec((tm, k), lambda j, i: (i, 0)),
                  _streamed_weight_spec(layer, k, tn, ni, nj, 0),
                  pl.BlockSpec((1, tn), lambda j, i: (0, jnp.maximum(j - 1, 0)))],
        out_specs=_streamed_out_spec(tm, tn),
        scratch_shapes=[pltpu.VMEM((2, k, tn), BF16)],
        compiler_params=_params(("arbitrary", "arbitrary")),
        name="matmul_bias_gelu",
    )(a, w, b.reshape(1, n))


def _ones_augmented(v):
    return jnp.concatenate([v, jnp.ones_like(v)], axis=1)


MOBA_HEADS_PER_STEP = 8
MOBA_BLOCKS_PER_ITER = 2


MOBA_VT_ROWS = HEAD_DIM + 16


def _column_max(x):
    rows, cols = x.shape
    x = x.reshape(rows // SUBLANES, SUBLANES, cols)
    while x.shape[0] > 1:
        half = x.shape[0] // 2
        x = jnp.maximum(x[:half], x[half:])
    return jnp.max(x[0], axis=0, keepdims=True)


def _moba_kernel(q_ref, k_ref, v_ref, o_ref, kmean_ref, vt_ref, *, n_blocks, heads):
    i = pl.program_id(2)
    blk = MOBA_BLOCK
    per_iter = MOBA_BLOCKS_PER_ITER

    @pl.when(i == 0)
    def _():
        for h in range(heads):
            hs = _head_slice(h)
            kf = k_ref[:, hs].astype(F32).reshape(n_blocks, blk, HEAD_DIM)
            kmean_ref[h] = jnp.mean(kf, axis=1)
            for n in range(n_blocks):
                vt = v_ref[n * blk:(n + 1) * blk, hs].astype(F32).T
                vt_ref[h, n, 0:HEAD_DIM, :] = vt.astype(BF16)
                vt_ref[h, n, HEAD_DIM:MOBA_VT_ROWS, :] = jnp.ones((MOBA_VT_ROWS - HEAD_DIM, blk), BF16)

    key_row = lax.broadcasted_iota(jnp.int32, (blk, blk), 0)
    query_col = lax.broadcasted_iota(jnp.int32, (blk, blk), 1)
    causal = key_row <= query_col
    blk_idx = lax.broadcasted_iota(jnp.int32, (n_blocks, blk), 0)
    own = pl.ds(pl.multiple_of(i * blk, blk), blk)

    def gate_scores(q, h):
        km = kmean_ref[h]
        km_hi = km.astype(BF16)
        km_lo = (km - km_hi.astype(F32)).astype(BF16)
        return _dot_nt(km_hi, q) + _dot_nt(km_lo, q)

    def select_bits(gate):
        rank = jnp.zeros(gate.shape, jnp.int32)
        for m in range(n_blocks):
            gm = gate[m:m + 1, :]
            beats = (gm > gate) | ((gm == gate) & (m < blk_idx))
            rank = rank + jnp.where(beats & (m < i), 1, 0)
        selected = (blk_idx < i) & (rank < MOBA_TOPK)
        return jnp.sum(jnp.where(selected, jnp.left_shift(1, blk_idx), 0), axis=0, keepdims=True)

    qs = [q_ref[:, _head_slice(h)] for h in range(heads)]
    gates = [gate_scores(qs[h], h) for h in range(heads)]
    own_scores = [_dot_nt(k_ref[own, _head_slice(h)], qs[h]) for h in range(heads)]
    bits = [select_bits(gates[h]) for h in range(heads)]

    states = []
    for h in range(heads):
        st = jnp.where(causal, own_scores[h], MASK_VALUE)
        m = _column_max(st)
        p = jnp.exp2(st - m).astype(BF16)
        states.append((m, _dot(vt_ref[h, i], p)))

    def body(c, states):
        rows = pl.ds(pl.multiple_of(c * (per_iter * blk), per_iter * blk), per_iter * blk)
        new_states = []
        scores = [_dot_nt(k_ref[rows, _head_slice(h)], qs[h]) for h in range(heads)]
        for h in range(heads):
            m, acc = states[h]
            st = scores[h]
            pieces = []
            for t in range(per_iter):
                keep = jnp.left_shift(bits[h], 31 - (c * per_iter + t)) < 0
                pieces.append(jnp.where(keep, st[t * blk:(t + 1) * blk], MASK_VALUE))
            st = jnp.concatenate(pieces, axis=0)
            m_new = jnp.maximum(m, _column_max(st))
            p = jnp.exp2(st - m_new).astype(BF16)
            acc = jnp.exp2(m - m_new) * acc
            for t in range(per_iter):
                acc = acc + _dot(vt_ref[h, c * per_iter + t], p[t * blk:(t + 1) * blk])
            new_states.append((m_new, acc))
        return tuple(new_states)

    n_iters = (i + per_iter - 1) // per_iter
    states = lax.fori_loop(0, n_iters, body, tuple(states))
    for h in range(heads):
        _, acc = states[h]
        out_t = acc[:HEAD_DIM] / acc[HEAD_DIM:HEAD_DIM + 1]
        o_ref[:, _head_slice(h)] = out_t.T.astype(o_ref.dtype)


def moba_attention(qkv, bsz, seq, n_heads, q_head0, k_head0, v_head0, n_out_heads):
    nb = seq // MOBA_BLOCK
    hp = MOBA_HEADS_PER_STEP
    assert n_heads % hp == 0 and q_head0 % hp == 0 and k_head0 % hp == 0 and v_head0 % hp == 0
    assert nb % MOBA_BLOCKS_PER_ITER == 0 and nb <= 31
    w = hp * HEAD_DIM
    return pl.pallas_call(
        functools.partial(_moba_kernel, n_blocks=nb, heads=hp),
        out_shape=jax.ShapeDtypeStruct((bsz * seq, n_out_heads * HEAD_DIM), BF16),
        grid=(bsz, n_heads // hp, nb),
        in_specs=[pl.BlockSpec((MOBA_BLOCK, w), lambda b, h, i: (b * nb + i, q_head0 // hp + h)),
                  pl.BlockSpec((seq, w), lambda b, h, i: (b, k_head0 // hp + h)),
                  pl.BlockSpec((seq, w), lambda b, h, i: (b, v_head0 // hp + h))],
        out_specs=pl.BlockSpec((MOBA_BLOCK, w), lambda b, h, i: (b * nb + i, h)),
        scratch_shapes=[pltpu.VMEM((hp, nb, HEAD_DIM), F32),
                        pltpu.VMEM((hp, nb, MOBA_VT_ROWS, MOBA_BLOCK), BF16)],
        compiler_params=_params(("parallel", "parallel", "arbitrary")),
        name="moba_attention",
    )(qkv, qkv, qkv)


DIL_TILE = 256
DIL_BACK = 128


def _residue_permutation(dil):
    per = DIL_TILE // dil
    r = lax.broadcasted_iota(jnp.int32, (DIL_TILE, DIL_TILE), 0)
    c = lax.broadcasted_iota(jnp.int32, (DIL_TILE, DIL_TILE), 1)
    return (c == (r % per) * dil + r // per).astype(BF16)


DIL_TILES_PER_ITER = 8


def _band_tiles(tiles):
    scores = [_dot_nt(q, k) for q, k, _, _, _ in tiles]
    results = []
    for s, (_, k, vaug, q0, k0) in zip(scores, tiles):
        n = k.shape[0]
        dist = (q0 - k0) + (lax.broadcasted_iota(jnp.int32, (DIL_TILE, n), 0)
                            - lax.broadcasted_iota(jnp.int32, (DIL_TILE, n), 1))
        s = jnp.where((dist >= 0) & (dist <= DIL_BACK), s, MASK_VALUE)
        m = jnp.max(s, axis=1, keepdims=True)
        p = jnp.exp2(s - m)
        ol = _dot(p.astype(vaug.dtype), vaug)
        l = ol[:, HEAD_DIM:]
        results.append((ol[:, :HEAD_DIM] / l, m + jnp.log(l) * LOG2_E))
    return results


def _dilated_kernel(q0_ref, q1_ref, q2_ref, k0_ref, k1_ref, k2_ref, v0_ref, v1_ref, v2_ref, _buf_ref,
                    o_ref, v0aug_ref, qs1_ref, ks1_ref, vs1_ref, qs2_ref, ks2_ref, vs2_ref, out_ref, lse_ref, *, seq):
    tile = DIL_TILE
    n_tiles = seq // tile
    keys = tile + DIL_BACK
    groups = ((q1_ref, k1_ref, v1_ref, qs1_ref, ks1_ref, vs1_ref),
              (q2_ref, k2_ref, v2_ref, qs2_ref, ks2_ref, vs2_ref))

    v0aug_ref[...] = _ones_augmented(v0_ref[...])
    for (_, dil), refs in zip(B_PATTERNS[1:], groups):
        per = tile // dil
        perm = _residue_permutation(dil)

        def gather(it, carry, per=per, perm=perm, refs=refs, dil=dil):
            q_ref, k_ref, v_ref, qs_ref, ks_ref, vs_ref = refs
            blocks = [it * DIL_TILES_PER_ITER + u for u in range(DIL_TILES_PER_ITER)]
            moved = []
            for b in blocks:
                rows = pl.ds(pl.multiple_of(b * tile, tile), tile)
                qkv = jnp.concatenate([q_ref[rows, :], k_ref[rows, :], v_ref[rows, :]], axis=1)
                moved.append(_dot(perm, qkv).astype(BF16))
            for b, mv in zip(blocks, moved):
                dst = pl.ds(pl.multiple_of(b * per, per), per)
                qs_ref[:, dst, :] = mv[:, _head_slice(0)].reshape(dil, per, HEAD_DIM)
                ks_ref[:, dst, :] = mv[:, _head_slice(1)].reshape(dil, per, HEAD_DIM)
                vs_ref[:, dst, :] = _ones_augmented(mv[:, _head_slice(2)]).reshape(dil, per, 2 * HEAD_DIM)
            return carry

        lax.fori_loop(0, n_tiles // DIL_TILES_PER_ITER, gather, 0)

    def window(t0):
        k0 = jnp.maximum(t0 - DIL_BACK, 0)
        return k0, pl.ds(pl.multiple_of(k0, DIL_BACK), keys)

    per_iter = DIL_TILES_PER_ITER

    def tiles_g0(it, carry):
        tiles, dsts = [], []
        for u in range(per_iter):
            t0 = pl.multiple_of((it * per_iter + u) * tile, tile)
            k0, krows = window(t0)
            tiles.append((q0_ref[pl.ds(t0, tile), :], k0_ref[krows, :], v0aug_ref[krows, :], t0, k0))
            dsts.append(pl.ds(t0, tile))
        for dst, (out, lse) in zip(dsts, _band_tiles(tiles)):
            out_ref[0, dst, :] = out
            lse_ref[0, dst, :] = lse
        return carry

    lax.fori_loop(0, n_tiles // per_iter, tiles_g0, 0)

    for g, ((_, dil), refs) in enumerate(zip(B_PATTERNS[1:], groups), start=1):
        sub_tiles = n_tiles // dil

        def tiles_g(it, carry, g=g, dil=dil, sub_tiles=sub_tiles, refs=refs[3:]):
            qs_ref, ks_ref, vs_ref = refs
            tiles, dsts = [], []
            for u in range(per_iter):
                n = it * per_iter + u
                res, ti = n // sub_tiles, n % sub_tiles
                t0 = pl.multiple_of(ti * tile, tile)
                if sub_tiles == 1:
                    k0, krows = 0, pl.ds(0, tile)
                else:
                    k0, krows = window(t0)
                tiles.append((qs_ref[res, pl.ds(t0, tile), :], ks_ref[res, krows, :], vs_ref[res, krows, :],
                              t0, k0))
                dsts.append(pl.ds(t0 * dil + res, tile, stride=dil))
            for dst, (out, lse) in zip(dsts, _band_tiles(tiles)):
                out_ref[g, dst, :] = out
                lse_ref[g, dst, :] = lse
            return carry

        lax.fori_loop(0, n_tiles // per_iter, tiles_g, 0)

    def merge(i, carry):
        rows = pl.ds(pl.multiple_of(i * tile, tile), tile)
        lses = [lse_ref[g, rows, :] for g in range(len(B_PATTERNS))]
        top = functools.reduce(jnp.maximum, lses)
        ws = [jnp.exp2(lse - top) for lse in lses]
        num = sum(w * out_ref[g, rows, :] for g, w in enumerate(ws))
        o_ref[rows, :] = (num / sum(ws)).astype(o_ref.dtype)
        return carry

    lax.fori_loop(0, n_tiles, merge, 0)


def dilated_attention(qkv, out_buf, bsz, seq, q_head0, k_head0, v_head0, out_head0):
    hg = N_HEADS_B_GROUP
    ng = len(B_PATTERNS)
    tile = DIL_TILE
    assert all(w // d == DIL_BACK and tile % d == 0 and (seq // tile) % d == 0 for w, d in B_PATTERNS)
    assert B_PATTERNS[0][1] == 1 and seq % tile == 0 and seq >= tile + DIL_BACK

    def spec(head0, g):
        return pl.BlockSpec((seq, HEAD_DIM), lambda b, h, g=g: (b, head0 + g * hg + h))

    scratch = [pltpu.VMEM((seq, 2 * HEAD_DIM), BF16)]
    for _, dil in B_PATTERNS[1:]:
        scratch += [pltpu.VMEM((dil, seq // dil, HEAD_DIM), BF16), pltpu.VMEM((dil, seq // dil, HEAD_DIM), BF16),
                    pltpu.VMEM((dil, seq // dil, 2 * HEAD_DIM), BF16)]
    scratch += [pltpu.VMEM((ng, seq, HEAD_DIM), F32), pltpu.VMEM((ng, seq, HEAD_DIM), F32)]
    return pl.pallas_call(
        functools.partial(_dilated_kernel, seq=seq),
        out_shape=jax.ShapeDtypeStruct(out_buf.shape, out_buf.dtype),
        grid=(bsz, hg),
        in_specs=([spec(q_head0, g) for g in range(ng)] + [spec(k_head0, g) for g in range(ng)]
                  + [spec(v_head0, g) for g in range(ng)] + [pl.BlockSpec(memory_space=pl.ANY)]),
        out_specs=pl.BlockSpec((seq, HEAD_DIM), lambda b, h: (b, out_head0 + h)),
        scratch_shapes=scratch,
        input_output_aliases={3 * ng: 0},
        compiler_params=_params(("parallel", "parallel")),
        name="dilated_attention",
    )(*([qkv] * (3 * ng)), out_buf)


def _sgu_kernel(u_ref, v_ref, gain_ref, bias_ref, ws_ref, bs_ref, o_ref):
    v = v_ref[...].astype(F32)
    mu = jnp.mean(v, axis=-1, keepdims=True)
    vc = v - mu
    vn = vc * lax.rsqrt(jnp.mean(vc * vc, axis=-1, keepdims=True) + NORM_EPS)
    vn = (vn * gain_ref[...] + bias_ref[...]).astype(BF16)
    c = GMLP_CHUNK
    row = lax.broadcasted_iota(jnp.int32, (c, c), 0)
    col = lax.broadcasted_iota(jnp.int32, (c, c), 1)
    causal = col <= row
    for g in range(ws_ref.shape[0]):
        sl = slice(g * GMLP_GROUP_DIM, (g + 1) * GMLP_GROUP_DIM)
        w = jnp.where(causal, ws_ref[g], 0.0).astype(BF16)
        f = _dot(w, vn[:, sl]) + bs_ref[:, g:g + 1]
        o_ref[:, sl] = (u_ref[:, sl].astype(F32) * f).astype(o_ref.dtype)


def spatial_gating(z, gain, bias, w_s, b_s):
    t, e2 = z.shape
    e = e2 // 2
    groups = w_s.shape[0]
    c = GMLP_CHUNK
    return pl.pallas_call(
        _sgu_kernel,
        out_shape=jax.ShapeDtypeStruct((t, e), BF16),
        grid=(t // c,),
        in_specs=[pl.BlockSpec((c, e), lambda i: (i, 0)),
                  pl.BlockSpec((c, e), lambda i: (i, 1)),
                  pl.BlockSpec((1, e), lambda i: (0, 0)),
                  pl.BlockSpec((1, e), lambda i: (0, 0)),
                  pl.BlockSpec((groups, c, c), lambda i: (0, 0, 0)),
                  pl.BlockSpec((c, groups), lambda i: (0, 0))],
        out_specs=pl.BlockSpec((c, e), lambda i: (i, 0)),
        compiler_params=_params(("parallel",)),
        name="spatial_gating",
    )(z, z, gain.reshape(1, e), bias.reshape(1, e), w_s, b_s.T)


def _causal_conv(a, prev, cw, cb):
    tm, n = a.shape
    a3 = a.reshape(tm // SUBLANES, SUBLANES, n)
    row_in_group = lax.broadcasted_iota(jnp.int32, (1, SUBLANES, n), 1)
    out = cb + a * cw[CONV_WIDTH - 1:CONV_WIDTH]
    for back in range(1, CONV_WIDTH):
        rot = pltpu.roll(a3, back, axis=1)
        before = jnp.concatenate([pltpu.roll(prev, back, axis=0)[None], rot[:-1]], axis=0)
        shifted = jnp.where(row_in_group < back, before, rot).reshape(tm, n)
        out = out + shifted * cw[CONV_WIDTH - 1 - back:CONV_WIDTH - back]
    return out


def _streamed_weight_spec(layer, k, tn, ni, nj, col0):
    assert k % ni == 0
    return pl.BlockSpec((None, k // ni, tn), lambda j, i: (layer, i, col0 + jnp.minimum(j, nj - 1)))


FFN_SUBTILE_ROWS = 256


def _streamed_out_spec(tm, tn):
    return pl.BlockSpec((tm, tn), lambda j, i: (jnp.where(j == 0, 0, i), jnp.maximum(j - 1, 0)))


def _stream_weight_rows(*refs):
    j, i = pl.program_id(0), pl.program_id(1)
    fill = j % 2
    for w32_ref, w_ref in zip(refs[0::2], refs[1::2]):
        chunk = w32_ref.shape[0]
        w_ref[fill, pl.ds(pl.multiple_of(i * chunk, chunk), chunk), :] = w32_ref[...].astype(w_ref.dtype)
    return j - 1, 1 - fill


def _ffn_up_kernel(h_ref, wg32_ref, wu32_ref, cwg_ref, cwu_ref, cbg_ref, cbu_ref, o_ref, pg_ref, pu_ref,
                   wg_ref, wu_ref, ag_ref, au_ref, *, tiles_per_seq, chunk, rows):
    i = pl.program_id(1)
    tm, tn = o_ref.shape
    col, cur = _stream_weight_rows(wg32_ref, wg_ref, wu32_ref, wu_ref)

    @pl.when(i % tiles_per_seq == 0)
    def _():
        pg_ref[...] = jnp.zeros(pg_ref.shape, F32)
        pu_ref[...] = jnp.zeros(pu_ref.shape, F32)

    @pl.when(col >= 0)
    def _():
        tiles = [(c0, r0) for c0 in range(0, tn, chunk) for r0 in range(0, tm, rows)]

        def matmuls(n):
            c0, r0 = tiles[n]
            sl, rs = slice(c0, c0 + chunk), slice(r0, r0 + rows)
            ag_ref[(n + cur) % 2] = _dot(h_ref[rs, :], wg_ref[cur, :, sl])
            au_ref[(n + cur) % 2] = _dot(h_ref[rs, :], wu_ref[cur, :, sl])

        matmuls(0)
        prev_g = prev_u = None
        for n, (c0, r0) in enumerate(tiles):
            sl, rs = slice(c0, c0 + chunk), slice(r0, r0 + rows)
            if n + 1 < len(tiles):
                matmuls(n + 1)
            if r0 == 0:
                prev_g, prev_u = pg_ref[:, sl], pu_ref[:, sl]
            ag, au = ag_ref[(n + cur) % 2], au_ref[(n + cur) % 2]
            gate = _causal_conv(ag, prev_g, cwg_ref[:, sl], cbg_ref[:, sl])
            up = _causal_conv(au, prev_u, cwu_ref[:, sl], cbu_ref[:, sl])
            o_ref[rs, sl] = (jax.nn.silu(gate) * up).astype(o_ref.dtype)
            prev_g, prev_u = ag[rows - SUBLANES:rows], au[rows - SUBLANES:rows]
            if r0 + rows == tm:
                pg_ref[:, sl] = prev_g
                pu_ref[:, sl] = prev_u


def ffn_up(h, w_up, layer, conv_w, conv_b, seq, tm=1024, tn=512, chunk=MXU_COLS, rows=FFN_SUBTILE_ROWS):
    t, k = h.shape
    f = w_up.shape[2] // 2
    tm, tn = min(tm, seq), min(tn, f)
    chunk, rows = min(chunk, tn), min(rows, tm)
    ni, nj = t // tm, f // tn
    assert seq % tm == 0 and f % tn == 0 and tn % chunk == 0 and tm % rows == 0
    cb = conv_b.reshape(1, 2 * f)

    def col_spec(rows_, col0):
        return pl.BlockSpec((rows_, tn), lambda j, i: (0, col0 + jnp.maximum(j - 1, 0)))

    return pl.pallas_call(
        functools.partial(_ffn_up_kernel, tiles_per_seq=seq // tm, chunk=chunk, rows=rows),
        out_shape=jax.ShapeDtypeStruct((t, f), BF16),
        grid=(nj + 1, ni),
        in_specs=[pl.BlockSpec((tm, k), lambda j, i: (i, 0)),
                  _streamed_weight_spec(layer, k, tn, ni, nj, 0),
                  _streamed_weight_spec(layer, k, tn, ni, nj, nj),
                  col_spec(CONV_WIDTH, 0), col_spec(CONV_WIDTH, nj),
                  col_spec(1, 0), col_spec(1, nj)],
        out_specs=_streamed_out_spec(tm, tn),
        scratch_shapes=[pltpu.VMEM((SUBLANES, tn), F32), pltpu.VMEM((SUBLANES, tn), F32),
                        pltpu.VMEM((2, k, tn), BF16), pltpu.VMEM((2, k, tn), BF16),
                        pltpu.VMEM((2, rows, chunk), F32), pltpu.VMEM((2, rows, chunk), F32)],
        compiler_params=_params(("arbitrary", "arbitrary")),
        name="ffn_up_conv_act",
    )(h, w_up, w_up, conv_w, conv_w, cb, cb)


def conv_ffn_block(x, norm_g, w_up, w_down, layer, conv_w, conv_b, seq):
    h = rmsnorm(x, norm_g, BF16)
    act = ffn_up(h, w_up, layer, conv_w, conv_b, seq)
    return matmul_residual(act, w_down, layer, x, tm=1024, tn=256, a_buffers=1)


def hybrid_attention_block(x, positions, norm_g, w_in, w_out, layer, bsz, seq):
    h = rmsnorm(x, norm_g, BF16)
    cos, sin_lo, sin_hi = rotary_tables(positions)
    nh = N_HEADS_QKV
    qkv = qkv_projection(h, w_in, layer, cos, sin_lo, sin_hi,
                         n_q_cols=nh * HEAD_DIM, n_rot_cols=2 * nh * HEAD_DIM)
    o = moba_attention(qkv, bsz, seq, N_HEADS_A, 0, nh, 2 * nh, N_HEADS_A + N_HEADS_B_GROUP)
    o = dilated_attention(qkv, o, bsz, seq, N_HEADS_A, nh + N_HEADS_A, 2 * nh + N_HEADS_A, N_HEADS_A)
    return matmul_residual(o, w_out, layer, x, tm=1024, tn=1024)


def spatial_gating_block(x, norm_g, w_in, w_out, layer, b_in, v_gain, v_bias, w_s, b_s):
    h = rmsnorm(x, norm_g, BF16)
    z = matmul_bias_gelu(h, w_in, layer, b_in)
    gated = spatial_gating(z, v_gain, v_bias, w_s, b_s)
    return matmul_residual(gated, w_out, layer, x, tm=1024, tn=1024)


def kernel(x, positions, attn_norm, attn_w_in, attn_w_out, sg_norm, sg_w_in, sg_b_in, sg_v_gain,
           sg_v_bias, sg_w_s, sg_b_s, sg_w_out, ffn_norm, ffn_w_up, ffn_conv_w, ffn_conv_b,
           ffn_w_down, final_norm):
    bsz, seq, d = x.shape
    depth = ffn_norm.shape[0]
    attn_w_out, sg_w_out, ffn_w_down = (w.astype(BF16) for w in (attn_w_out, sg_w_out, ffn_w_down))
    h = x.reshape(bsz * seq, d)
    for layer in range(depth):
        i = layer // 2
        if layer % 2 == 0:
            h = hybrid_attention_block(h, positions, attn_norm[i], attn_w_in, attn_w_out, i, bsz, seq)
        else:
            h = spatial_gating_block(h, sg_norm[i], sg_w_in, sg_w_out, i, sg_b_in[i], sg_v_gain[i],
                                     sg_v_bias[i], sg_w_s[i], sg_b_s[i])
        h = conv_ffn_block(h, ffn_norm[layer], ffn_w_up, ffn_w_down, layer, ffn_conv_w[layer],
                           ffn_conv_b[layer], seq)
    return rmsnorm(h, final_norm, x.dtype).reshape(bsz, seq, d)
```
